```python
import math, functools
import jax, jax.numpy as jnp
from jax import lax
import numpy as np

D_MODEL = 1024
BATCH = 4
SEQ = 8192
DEPTH = 1
DEC_BATCH = 128
DEC_SEQ = 1
PAST_LEN = 8192
PAGE_SIZE = 128

D_MIX = D_MODEL
ATT_WIDTH = D_MIX // 2
N_ATT_HEADS = 4
HEAD_DIM = ATT_WIDTH // (2 * N_ATT_HEADS)
V_DIM = 2 * HEAD_DIM
RG_WIDTH = D_MIX - ATT_WIDTH
RG_HEADS = 8
RG_HD = RG_WIDTH // RG_HEADS
CONV_W = 4
RG_C = 8.0
D_IN = 3 * ATT_WIDTH + 2 * RG_WIDTH
N_EXPERTS = 64
TOP_K = 8
N_GROUPS = 8
TOPK_GROUPS = 4
D_EXPERT = 256
D_SHARED = 256
ROUTE_SCALE = 2.5
PLE_DIM = 256
Q_BLOCK = 128
MOE_BLOCK = 128
EPS = 1e-6
ATT_SCALE = HEAD_DIM ** -0.5
NEG_INF = -1e30

kernel_name = 'hymba_diffattn_rglru_moe_ple_step'


def rmsnorm(x, g):
    xf = x.astype(jnp.float32)
    y = xf * lax.rsqrt(jnp.mean(xf * xf, axis=-1, keepdims=True) + EPS)
    return (y * g.astype(jnp.float32)).astype(x.dtype)


def alibi_slopes():
    return jnp.exp2(-8.0 * jnp.arange(1, N_ATT_HEADS + 1, dtype=jnp.float32) / N_ATT_HEADS)


def diff_attention(q, q_pos, chunks, lam):
    slopes = alibi_slopes()[:, None, None, None]
    scores = []
    for k, v, k_pos in chunks:
        s = jnp.einsum('bqhcd,bkhcd->bhcqk', q, k, preferred_element_type=jnp.float32) * ATT_SCALE
        dist = (q_pos[:, None] - k_pos[None, :]).astype(jnp.float32)
        scores.append(jnp.where(dist >= 0, s - slopes * dist, NEG_INF))
    p = jax.nn.softmax(jnp.concatenate(scores, axis=-1), axis=-1)
    w = p[:, :, 0] - lam * p[:, :, 1]
    out = None
    off = 0
    for k, v, k_pos in chunks:
        tk = k.shape[1]
        o = jnp.einsum('bhqk,bkhe->bqhe', w[..., off:off + tk], v.astype(jnp.float32))
        out = o if out is None else out + o
        off += tk
    return out


def prompt_attention(q, k, v, lam):
    B, S = q.shape[0], q.shape[1]
    nqb = S // Q_BLOCK
    pos = jnp.arange(S)
    qb = q.reshape(B, nqb, Q_BLOCK, N_ATT_HEADS, 2, HEAD_DIM).swapaxes(0, 1)
    pb = pos.reshape(nqb, Q_BLOCK)

    def blk(args):
        q_i, p_i = args
        return diff_attention(q_i, p_i, [(k, v, pos)], lam)

    out = lax.map(blk, (qb, pb))
    return out.swapaxes(0, 1).reshape(B, S, N_ATT_HEADS, V_DIM)


def sample_attention(q, k, v, lam, cache_k, cache_v, page_table, layer_idx):
    T = q.shape[1]
    past = page_table.shape[1] * PAGE_SIZE
    q_pos = past + jnp.arange(T)
    past_pos = jnp.arange(past)

    def one(args):
        q_b, k_b, v_b, pt_b = args
        k_past = cache_k[layer_idx, pt_b].reshape(1, past, N_ATT_HEADS, 2, HEAD_DIM)
        v_past = cache_v[layer_idx, pt_b].reshape(1, past, N_ATT_HEADS, V_DIM)
        o = diff_attention(q_b[None], q_pos,
                           [(k_past, v_past, past_pos), (k_b[None], v_b[None], q_pos)], lam)
        return o[0]

    return lax.map(one, (q, k, v, page_table))


def causal_conv(xb, buf, w, b):
    T = xb.shape[1]
    xx = jnp.concatenate([buf.astype(xb.dtype), xb], axis=1)
    y = b
    for j in range(CONV_W):
        y = y + xx[:, j:j + T] * w[j]
    return y, xx[:, -(CONV_W - 1):]


def rglru(xc, h0, w_a, b_a, w_x, b_x, lam_rg):
    Bx, T, _ = xc.shape
    xf = xc.astype(jnp.float32)
    xh = xf.reshape(Bx, T, RG_HEADS, RG_HD)
    r = jax.nn.sigmoid(jnp.einsum('bthi,hij->bthj', xh, w_a.astype(jnp.float32)).reshape(Bx, T, RG_WIDTH) + b_a.astype(jnp.float32))
    ig = jax.nn.sigmoid(jnp.einsum('bthi,hij->bthj', xh, w_x.astype(jnp.float32)).reshape(Bx, T, RG_WIDTH) + b_x.astype(jnp.float32))
    log_a = -RG_C * r * jax.nn.softplus(-lam_rg.astype(jnp.float32))
    a = jnp.exp(log_a)
    bterm = jnp.sqrt(-jnp.expm1(2.0 * log_a)) * (ig * xf)
    bterm = bterm.at[:, 0].add(a[:, 0] * h0.astype(jnp.float32))

    def comb(l, rr):
        return (l[0] * rr[0], rr[0] * l[1] + rr[1])

    _, h = lax.associative_scan(comb, (a, bterm), axis=1)
    return h, h[:, -1]


def moe(x2, w_router, router_bias, w_e1, w_e3, w_e2, w_s1, w_s3, w_s2):
    T, D = x2.shape
    scores = jax.nn.sigmoid(jnp.einsum('td,de->te', x2, w_router, preferred_element_type=jnp.float32))
    choice = scores + router_bias.astype(jnp.float32)
    grp_score = lax.top_k(choice.reshape(T, N_GROUPS, N_EXPERTS // N_GROUPS), 2)[0].sum(-1)
    _, gidx = lax.top_k(grp_score, TOPK_GROUPS)
    gmask = jnp.any(gidx[..., None] == jnp.arange(N_GROUPS), axis=1)
    emask = jnp.repeat(gmask, N_EXPERTS // N_GROUPS, axis=1)
    _, idx = lax.top_k(jnp.where(emask, choice, NEG_INF), TOP_K)
    g = jnp.take_along_axis(scores, idx, axis=1)
    g = g / jnp.sum(g, axis=-1, keepdims=True) * ROUTE_SCALE
    N = T * TOP_K
    e_flat = idx.reshape(-1)
    tok_flat = jnp.arange(N) // TOP_K
    g_flat = g.reshape(-1)
    order = jnp.argsort(e_flat)
    e_sorted = e_flat[order]
    counts = jnp.bincount(e_flat, length=N_EXPERTS)
    padded = ((counts + MOE_BLOCK - 1) // MOE_BLOCK) * MOE_BLOCK
    pad_end = jnp.cumsum(padded)
    pad_start = pad_end - padded
    start = jnp.cumsum(counts) - counts
    dest = pad_start[e_sorted] + (jnp.arange(N) - start[e_sorted])
    nb = -(-N // MOE_BLOCK) + N_EXPERTS
    P = nb * MOE_BLOCK
    buf_tok = jnp.full((P,), T, dtype=jnp.int32).at[dest].set(tok_flat[order].astype(jnp.int32))
    buf_gate = jnp.zeros((P,), jnp.float32).at[dest].set(g_flat[order])
    block_expert = jnp.minimum(jnp.searchsorted(pad_end, jnp.arange(nb) * MOE_BLOCK, side='right'), N_EXPERTS - 1)
    x_pad = jnp.concatenate([x2, jnp.zeros((1, D), x2.dtype)], axis=0)

    def expert_block(args):
        tok_b, e_b = args
        xb = x_pad[tok_b]
        hdn = jax.nn.silu(xb @ w_e1[e_b]) * (xb @ w_e3[e_b])
        return hdn @ w_e2[e_b]

    out = lax.map(expert_block, (buf_tok.reshape(nb, MOE_BLOCK), block_expert)).reshape(P, D)
    routed = jnp.zeros((T + 1, D), jnp.float32).at[buf_tok].add(out.astype(jnp.float32) * buf_gate[:, None])[:T]
    shared = (jax.nn.silu(x2 @ w_s1) * (x2 @ w_s3)) @ w_s2
    return (routed + shared.astype(jnp.float32)).astype(x2.dtype)


def layer(x, p, attn_fn, conv_buf, h0, lam_init, lp):
    B, T, _ = x.shape
    h = rmsnorm(x, lp['g_norm1'])
    z = h @ lp['w_in']
    q, k, v, xr, gr = jnp.split(z, [ATT_WIDTH, 2 * ATT_WIDTH, 3 * ATT_WIDTH, 3 * ATT_WIDTH + RG_WIDTH], axis=-1)
    q = rmsnorm(q.reshape(B, T, N_ATT_HEADS, 2, HEAD_DIM), lp['g_q'])
    k = rmsnorm(k.reshape(B, T, N_ATT_HEADS, 2, HEAD_DIM), lp['g_k'])
    v = v.reshape(B, T, N_ATT_HEADS, V_DIM)
    f32 = jnp.float32
    lam = (jnp.exp(jnp.sum(lp['lam_q1'].astype(f32) * lp['lam_k1'].astype(f32)))
           - jnp.exp(jnp.sum(lp['lam_q2'].astype(f32) * lp['lam_k2'].astype(f32))) + lam_init)
    att = attn_fn(q, k, v, lam)
    att = rmsnorm(att, lp['g_subln']) * (1.0 - lam_init)
    xc, conv_new = causal_conv(xr, conv_buf, lp['conv_w'], lp['conv_b'])
    hr, h_last = rglru(xc, h0, lp['w_a'], lp['b_a'], lp['w_x'], lp['b_x'], lp['lam_rg'])
    rg = hr * jax.nn.gelu(gr.astype(f32))
    mixed = jnp.concatenate([att.reshape(B, T, ATT_WIDTH), rg], axis=-1).astype(x.dtype) @ lp['w_out']
    x = x + mixed
    hm = rmsnorm(x, lp['g_norm2']).reshape(B * T, D_MODEL)
    x = x + moe(hm, lp['w_router'], lp['router_bias'], lp['w_e1'], lp['w_e3'], lp['w_e2'],
                lp['w_s1'], lp['w_s3'], lp['w_s2']).reshape(B, T, D_MODEL)
    gate = jax.nn.sigmoid((rmsnorm(x, lp['g_ple']) @ lp['w_ple_gate']).astype(f32))
    x = x + (gate * (p @ lp['w_ple_proj']).astype(f32)).astype(x.dtype)
    return x, k, v, conv_new, h_last


def setup_inputs(seed: int = 0) -> dict:
    key = jax.random.key(seed)
    ks = iter(jax.random.split(key, 48))
    nrm = lambda shape, s: jax.random.normal(next(ks), shape, jnp.float32) * s
    gain = lambda shape: 1.0 + 0.02 * jax.random.normal(next(ks), shape, jnp.float32)
    n_pages = PAST_LEN // PAGE_SIZE
    n_pool = (DEC_BATCH * n_pages * 5) // 4
    page_table = jax.random.permutation(next(ks), n_pool)[:DEC_BATCH * n_pages].reshape(DEC_BATCH, n_pages).astype(jnp.int32)
    u = jax.random.uniform(next(ks), (DEPTH, RG_WIDTH), jnp.float32, 0.9, 0.999)
    a0 = u ** (1.0 / RG_C)
    lam_rg = jnp.log(a0) - jnp.log1p(-a0)
    return {
        'x_prompt': nrm((BATCH, SEQ, D_MODEL), 1.0),
        'x_sample': nrm((DEC_BATCH, DEC_SEQ, D_MODEL), 1.0),
        'cache_k': nrm((DEPTH, n_pool, PAGE_SIZE, N_ATT_HEADS, 2, HEAD_DIM), 1.0),
        'cache_v': nrm((DEPTH, n_pool, PAGE_SIZE, N_ATT_HEADS, V_DIM), 1.0),
        'state_conv': nrm((DEPTH, DEC_BATCH, CONV_W - 1, RG_WIDTH), 1.0),
        'state_h': nrm((DEPTH, DEC_BATCH, RG_WIDTH), 0.5),
        'page_table': page_table,
        'p_prompt': nrm((DEPTH, BATCH, SEQ, PLE_DIM), 1.0),
        'p_sample': nrm((DEPTH, DEC_BATCH, DEC_SEQ, PLE_DIM), 1.0),
        'g_norm1': gain((DEPTH, D_MODEL)),
        'w_in': nrm((DEPTH, D_MODEL, D_IN), D_MODEL ** -0.5),
        'g_q': gain((DEPTH, HEAD_DIM)),
        'g_k': gain((DEPTH, HEAD_DIM)),
        'lam_q1': nrm((DEPTH, HEAD_DIM), 0.1),
        'lam_k1': nrm((DEPTH, HEAD_DIM), 0.1),
        'lam_q2': nrm((DEPTH, HEAD_DIM), 0.1),
        'lam_k2': nrm((DEPTH, HEAD_DIM), 0.1),
        'g_subln': gain((DEPTH, V_DIM)),
        'conv_w': nrm((DEPTH, CONV_W, RG_WIDTH), CONV_W ** -0.5),
        'conv_b': nrm((DEPTH, RG_WIDTH), 0.01),
        'w_a': nrm((DEPTH, RG_HEADS, RG_HD, RG_HD), RG_HD ** -0.5),
        'b_a': nrm((DEPTH, RG_WIDTH), 0.01),
        'w_x': nrm((DEPTH, RG_HEADS, RG_HD, RG_HD), RG_HD ** -0.5),
        'b_x': nrm((DEPTH, RG_WIDTH), 0.01),
        'lam_rg': lam_rg,
        'w_out': nrm((DEPTH, D_MIX, D_MODEL), D_MIX ** -0.5),
        'g_norm2': gain((DEPTH, D_MODEL)),
        'w_router': nrm((DEPTH, D_MODEL, N_EXPERTS), D_MODEL ** -0.5),
        'router_bias': nrm((DEPTH, N_EXPERTS), 0.01),
        'w_e1': nrm((DEPTH, N_EXPERTS, D_MODEL, D_EXPERT), D_MODEL ** -0.5),
        'w_e3': nrm((DEPTH, N_EXPERTS, D_MODEL, D_EXPERT), D_MODEL ** -0.5),
        'w_e2': nrm((DEPTH, N_EXPERTS, D_EXPERT, D_MODEL), D_EXPERT ** -0.5),
        'w_s1': nrm((DEPTH, D_MODEL, D_SHARED), D_MODEL ** -0.5),
        'w_s3': nrm((DEPTH, D_MODEL, D_SHARED), D_MODEL ** -0.5),
        'w_s2': nrm((DEPTH, D_SHARED, D_MODEL), D_SHARED ** -0.5),
        'g_ple': gain((DEPTH, D_MODEL)),
        'w_ple_gate': nrm((DEPTH, D_MODEL, D_MODEL), D_MODEL ** -0.5),
        'w_ple_proj': nrm((DEPTH, PLE_DIM, D_MODEL), PLE_DIM ** -0.5),
    }


def reference(x_prompt, x_sample, cache_k, cache_v, state_conv, state_h, page_table, p_prompt, p_sample,
              g_norm1, w_in, g_q, g_k, lam_q1, lam_k1, lam_q2, lam_k2, g_subln, conv_w, conv_b,
              w_a, b_a, w_x, b_x, lam_rg, w_out, g_norm2, w_router, router_bias, w_e1, w_e3, w_e2,
              w_s1, w_s3, w_s2, g_ple, w_ple_gate, w_ple_proj):
    yp, ys = x_prompt, x_sample
    kp_l, vp_l, cp_l, hp_l = [], [], [], []
    ks_l, vs_l, cs_l, hs_l = [], [], [], []
    for i in range(DEPTH):
        lp = dict(g_norm1=g_norm1[i], w_in=w_in[i], g_q=g_q[i], g_k=g_k[i], lam_q1=lam_q1[i], lam_k1=lam_k1[i],
                  lam_q2=lam_q2[i], lam_k2=lam_k2[i], g_subln=g_subln[i], conv_w=conv_w[i], conv_b=conv_b[i],
                  w_a=w_a[i], b_a=b_a[i], w_x=w_x[i], b_x=b_x[i], lam_rg=lam_rg[i], w_out=w_out[i],
                  g_norm2=g_norm2[i], w_router=w_router[i], router_bias=router_bias[i], w_e1=w_e1[i],
                  w_e3=w_e3[i], w_e2=w_e2[i], w_s1=w_s1[i], w_s3=w_s3[i], w_s2=w_s2[i], g_ple=g_ple[i],
                  w_ple_gate=w_ple_gate[i], w_ple_proj=w_ple_proj[i])
        lam_init = 0.8 - 0.6 * math.exp(-0.3 * i)
        conv0 = jnp.zeros((yp.shape[0], CONV_W - 1, RG_WIDTH), yp.dtype)
        h0 = jnp.zeros((yp.shape[0], RG_WIDTH), jnp.float32)
        yp, k, v, c, h = layer(yp, p_prompt[i], prompt_attention, conv0, h0, lam_init, lp)
        kp_l.append(k); vp_l.append(v); cp_l.append(c); hp_l.append(h)
        attn_s = functools.partial(sample_attention, cache_k=cache_k, cache_v=cache_v,
                                   page_table=page_table, layer_idx=i)
        ys, k, v, c, h = layer(ys, p_sample[i], attn_s, state_conv[i], state_h[i], lam_init, lp)
        ks_l.append(k); vs_l.append(v); cs_l.append(c); hs_l.append(h)
    return (yp, ys, jnp.stack(kp_l), jnp.stack(vp_l), jnp.stack(cp_l), jnp.stack(hp_l),
            jnp.stack(ks_l), jnp.stack(vs_l), jnp.stack(cs_l), jnp.stack(hs_l))
```

```python
import functools
import math

import jax
import jax.numpy as jnp
from jax import lax
from jax.experimental import pallas as pl
from jax.experimental.pallas import tpu as pltpu

F32 = jnp.float32
BF16 = jnp.bfloat16

EPS = 1e-6
NEG_INF = -1e30
LANES = 128
SUBLANES = 8
N_HEADS = 4
HEAD_DIM = 64
V_DIM = 2 * HEAD_DIM
N_MAPS = 2 * N_HEADS
ATT_W = N_HEADS * V_DIM
RG_W = 512
RG_C = 8.0
CONV_W = 4
N_EXPERTS = 64
N_GROUPS = 8
GROUP_SIZE = N_EXPERTS // N_GROUPS
TOPK_GROUPS = 4
TOP_K = 8
ROUTE_SCALE = 2.5
ATT_SCALE = HEAD_DIM ** -0.5
PAGE = 128
VMEM_LIMIT = 56 * 1024 * 1024


def _cparams(sem):
    return pltpu.CompilerParams(dimension_semantics=sem, vmem_limit_bytes=VMEM_LIMIT)


def _sigmoid(x):
    return 1.0 / (1.0 + jnp.exp(-x))


def _nt_dot(a, b):
    return lax.dot_general(a, b, (((1,), (1,)), ((), ())), preferred_element_type=F32)


def _full(shape):
    n = len(shape)
    return pl.BlockSpec(shape, lambda *_: (0,) * n)


def _inproj_kernel(x_ref, g1_ref, w_ref, gq_ref, gk_ref, bd_ref,
                   qe_ref, k_ref, kb_ref, v_ref, vb_ref, xr_ref, gr_ref):
    x = x_ref[...]
    ms = jnp.mean(x * x, axis=-1, keepdims=True)
    h = (x * lax.rsqrt(ms + EPS) * g1_ref[...]).astype(BF16)

    def proj(lo, hi):
        return jnp.dot(h, w_ref[:, lo:hi], preferred_element_type=F32)

    def segnorm(z, g):
        sq = z * z
        hi = sq.astype(BF16)
        lo = (sq - hi.astype(F32)).astype(BF16)
        m = (jnp.dot(hi, bd_ref[...], preferred_element_type=F32)
             + jnp.dot(lo, bd_ref[...], preferred_element_type=F32))
        return z * lax.rsqrt(m + EPS) * g

    q = segnorm(proj(0, ATT_W), gq_ref[...]) * ATT_SCALE
    lane = lax.broadcasted_iota(jnp.int32, (1, V_DIM), 1)
    for hh in range(N_HEADS):
        qh = q[:, hh * V_DIM:(hh + 1) * V_DIM]
        for c in range(2):
            mm = 2 * hh + c
            sel = (lane >= HEAD_DIM) if c else (lane < HEAD_DIM)
            qe_ref[:, mm * V_DIM:(mm + 1) * V_DIM] = jnp.where(sel, qh, 0.0).astype(BF16)
    k = segnorm(proj(ATT_W, 2 * ATT_W), gk_ref[...])
    k_ref[...] = k
    kb_ref[...] = k.astype(BF16)
    v = proj(2 * ATT_W, 3 * ATT_W)
    v_ref[...] = v
    vb_ref[...] = v.astype(BF16)
    xr_ref[...] = proj(3 * ATT_W, 3 * ATT_W + RG_W)
    gr_ref[...] = proj(3 * ATT_W + RG_W, 3 * ATT_W + 2 * RG_W)


def _inproj(x2, g1, w_in_bf, gq_t, gk_t, bd):
    T, D = x2.shape
    tm = min(256, T)
    row = lambda w: pl.BlockSpec((tm, w), lambda i: (i, 0))
    out_shape = (
        jax.ShapeDtypeStruct((T, N_MAPS * V_DIM), BF16),
        jax.ShapeDtypeStruct((T, ATT_W), F32), jax.ShapeDtypeStruct((T, ATT_W), BF16),
        jax.ShapeDtypeStruct((T, ATT_W), F32), jax.ShapeDtypeStruct((T, ATT_W), BF16),
        jax.ShapeDtypeStruct((T, RG_W), F32), jax.ShapeDtypeStruct((T, RG_W), F32),
    )
    return pl.pallas_call(
        _inproj_kernel,
        grid=(T // tm,),
        in_specs=[row(D), _full(g1.shape), _full(w_in_bf.shape), _full(gq_t.shape),
                  _full(gk_t.shape), _full(bd.shape)],
        out_specs=(row(N_MAPS * V_DIM), row(ATT_W), row(ATT_W), row(ATT_W), row(ATT_W),
                   row(RG_W), row(RG_W)),
        out_shape=out_shape,
        compiler_params=_cparams(("parallel",)),
        name="inproj",
    )(x2, g1, w_in_bf, gq_t, gk_t, bd)


def _lam_value(lq1, lk1, lq2, lk2, lam_init):
    s1 = jnp.sum(lq1 * lk1, axis=-1, keepdims=True)
    s2 = jnp.sum(lq2 * lk2, axis=-1, keepdims=True)
    return jnp.exp(s1) - jnp.exp(s2) + lam_init


def _subln(o, g, lam_init):
    ms = jnp.mean(o * o, axis=-1, keepdims=True)
    return (o * lax.rsqrt(ms + EPS) * g) * (1.0 - lam_init)


def _prompt_attn_kernel(qe_ref, kb_ref, vb_ref, lq1_ref, lk1_ref, lq2_ref, lk2_ref, gs_ref,
                        o_ref, m_scr, l_scr, acc_scr, *, tq, tk, lam_init):
    i = pl.program_id(1)
    j = pl.program_id(2)
    qstart = i * tq
    kstart = j * tk
    jlast = (qstart + tq - 1) // tk

    @pl.when(j == 0)
    def _():
        m_scr[...] = jnp.full(m_scr.shape, NEG_INF, F32)
        l_scr[...] = jnp.zeros(l_scr.shape, F32)
        acc_scr[...] = jnp.zeros(acc_scr.shape, F32)

    @pl.when(j <= jlast)
    def _():
        qpos = qstart + lax.broadcasted_iota(jnp.int32, (tq, tk), 0)
        kpos = kstart + lax.broadcasted_iota(jnp.int32, (tq, tk), 1)
        dist_i = qpos - kpos
        causal = dist_i >= 0
        dist = dist_i.astype(F32)
        for hh in range(N_HEADS):
            slope = 2.0 ** (-8.0 * (hh + 1) / N_HEADS)
            bias = slope * dist
            kh = kb_ref[:, hh * V_DIM:(hh + 1) * V_DIM]
            vh = vb_ref[:, hh * V_DIM:(hh + 1) * V_DIM]
            for c in range(2):
                mm = 2 * hh + c
                s = _nt_dot(qe_ref[:, mm * V_DIM:(mm + 1) * V_DIM], kh)
                s = jnp.where(causal, s - bias, NEG_INF)
                m_prev = m_scr[mm]
                m_new = jnp.maximum(m_prev, jnp.max(s, axis=-1, keepdims=True))
                alpha = jnp.exp(m_prev - m_new)
                p = jnp.exp(s - m_new)
                l_scr[mm] = alpha * l_scr[mm] + jnp.sum(p, axis=-1, keepdims=True)
                acc_scr[mm] = alpha * acc_scr[mm] + jnp.dot(
                    p.astype(BF16), vh, preferred_element_type=F32)
                m_scr[mm] = m_new

    @pl.when(j == jlast)
    def _():
        lam = _lam_value(lq1_ref[...], lk1_ref[...], lq2_ref[...], lk2_ref[...], lam_init)
        for hh in range(N_HEADS):
            o1 = acc_scr[2 * hh] / l_scr[2 * hh]
            o2 = acc_scr[2 * hh + 1] / l_scr[2 * hh + 1]
            o = o1 - lam * o2
            o_ref[:, hh * V_DIM:(hh + 1) * V_DIM] = _subln(o, gs_ref[...], lam_init).astype(o_ref.dtype)


def _prompt_attention(qe, kb, vb, lq1, lk1, lq2, lk2, gs, batch, seq, lam_init):
    tq = min(256, seq)
    tk = min(512, seq)
    nq, nk = seq // tq, seq // tk

    def kv_map(b, i, j):
        jlast = (i * tq + tq - 1) // tk
        return (b * nk + jnp.minimum(j, jlast), 0)

    kern = functools.partial(_prompt_attn_kernel, tq=tq, tk=tk, lam_init=lam_init)
    return pl.pallas_call(
        kern,
        grid=(batch, nq, nk),
        in_specs=[pl.BlockSpec((tq, N_MAPS * V_DIM), lambda b, i, j: (b * nq + i, 0)),
                  pl.BlockSpec((tk, ATT_W), kv_map),
                  pl.BlockSpec((tk, ATT_W), kv_map),
                  _full(lq1.shape), _full(lk1.shape), _full(lq2.shape), _full(lk2.shape),
                  _full(gs.shape)],
        out_specs=pl.BlockSpec((tq, ATT_W), lambda b, i, j: (b * nq + i, 0)),
        out_shape=jax.ShapeDtypeStruct((batch * seq, ATT_W), BF16),
        scratch_shapes=[pltpu.VMEM((N_MAPS, tq, 1), F32), pltpu.VMEM((N_MAPS, tq, 1), F32),
                        pltpu.VMEM((N_MAPS, tq, V_DIM), F32)],
        compiler_params=_cparams(("parallel", "parallel", "arbitrary")),
        name="prompt_attn",
    )(qe, kb, vb, lq1, lk1, lq2, lk2, gs)


def _decode_attn_kernel(pt_ref, qb_ref, kn_ref, vn_ref, sl_ref, lq1_ref, lk1_ref, lq2_ref, lk2_ref,
                        gs_ref, *refs, pps, past, lam_init):
    kp_refs = refs[:pps]
    vp_refs = refs[pps:2 * pps]
    o_ref, m_scr, l_scr, acc_scr = refs[2 * pps:]
    g = pl.program_id(1)
    ng = pl.num_programs(1)

    @pl.when(g == 0)
    def _():
        m_scr[...] = jnp.full(m_scr.shape, NEG_INF, F32)
        l_scr[...] = jnp.zeros(l_scr.shape, F32)
        acc_scr[...] = jnp.zeros(acc_scr.shape, F32)

    q = qb_ref[...]
    slopes = sl_ref[...]
    lane = lax.broadcasted_iota(jnp.int32, (1, PAGE), 1)
    head_of_row = lax.broadcasted_iota(jnp.int32, (N_MAPS, 1), 0) // 2
    m_run = m_scr[...]
    l_run = l_scr[...]
    acc = acc_scr[...]
    for pi in range(pps):
        kpos = (g * pps + pi) * PAGE + lane
        dist = (past - kpos).astype(F32)
        s = jnp.dot(q, kp_refs[pi][...].astype(BF16), preferred_element_type=F32)
        s = s - slopes * dist
        m_new = jnp.maximum(m_run, jnp.max(s, axis=-1, keepdims=True))
        alpha = jnp.exp(m_run - m_new)
        p = jnp.exp(s - m_new)
        l_run = alpha * l_run + jnp.sum(p, axis=-1, keepdims=True)
        pb = p.astype(BF16)
        pv = jnp.zeros((N_MAPS, V_DIM), F32)
        for hh in range(N_HEADS):
            vh = vp_refs[pi][pl.ds(hh, PAGE, stride=N_HEADS), :].astype(BF16)
            pv = jnp.where(head_of_row == hh, jnp.dot(pb, vh, preferred_element_type=F32), pv)
        acc = alpha * acc + pv
        m_run = m_new
    m_scr[...] = m_run
    l_scr[...] = l_run
    acc_scr[...] = acc

    @pl.when(g == ng - 1)
    def _():
        kn = kn_ref[...].astype(BF16).astype(F32)
        vn = vn_ref[...].astype(BF16).astype(F32)
        s = jnp.sum(q.astype(F32) * kn, axis=-1, keepdims=True)
        m_new = jnp.maximum(m_run, s)
        alpha = jnp.exp(m_run - m_new)
        p = jnp.exp(s - m_new)
        l_fin = alpha * l_run + p
        acc_fin = alpha * acc + p.astype(BF16).astype(F32) * vn
        lam = _lam_value(lq1_ref[...], lk1_ref[...], lq2_ref[...], lk2_ref[...], lam_init)
        outn = acc_fin / l_fin
        for hh in range(N_HEADS):
            o = outn[2 * hh:2 * hh + 1] - lam * outn[2 * hh + 1:2 * hh + 2]
            o_ref[:, hh * V_DIM:(hh + 1) * V_DIM] = _subln(o, gs_ref[...], lam_init)


def _decode_attention(qblk, k_new, v_rows, slopes, lq1, lk1, lq2, lk2, gs, cache_kt, cache_v2,
                      page_table, lam_init):
    nb, n_pages = page_table.shape
    pps = 8
    while n_pages % pps:
        pps //= 2
    past = n_pages * PAGE
    pt_flat = page_table.reshape(-1)

    def page_map(pi):
        return lambda b, g, pt: (pt[b * n_pages + g * pps + pi], 0, 0)

    k_specs = [pl.BlockSpec((None, ATT_W, PAGE), page_map(pi)) for pi in range(pps)]
    v_specs = [pl.BlockSpec((None, PAGE * N_HEADS, V_DIM), page_map(pi)) for pi in range(pps)]
    cst = lambda shape: pl.BlockSpec(shape, lambda b, g, pt: (0,) * len(shape))
    kern = functools.partial(_decode_attn_kernel, pps=pps, past=past, lam_init=lam_init)
    grid_spec = pltpu.PrefetchScalarGridSpec(
        num_scalar_prefetch=1,
        grid=(nb, n_pages // pps),
        in_specs=[pl.BlockSpec((None, N_MAPS, ATT_W), lambda b, g, pt: (b, 0, 0)),
                  pl.BlockSpec((None, 1, ATT_W), lambda b, g, pt: (b, 0, 0)),
                  pl.BlockSpec((None, N_MAPS, V_DIM), lambda b, g, pt: (b, 0, 0)),
                  cst(slopes.shape), cst(lq1.shape), cst(lk1.shape), cst(lq2.shape), cst(lk2.shape),
                  cst(gs.shape)] + k_specs + v_specs,
        out_specs=pl.BlockSpec((None, 1, ATT_W), lambda b, g, pt: (b, 0, 0)),
        scratch_shapes=[pltpu.VMEM((N_MAPS, 1), F32), pltpu.VMEM((N_MAPS, 1), F32),
                        pltpu.VMEM((N_MAPS, V_DIM), F32)],
    )
    return pl.pallas_call(
        kern,
        grid_spec=grid_spec,
        out_shape=jax.ShapeDtypeStruct((nb, 1, ATT_W), F32),
        compiler_params=_cparams(("parallel", "arbitrary")),
        name="decode_attn",
    )(pt_flat, qblk, k_new, v_rows, slopes, lq1, lk1, lq2, lk2, gs,
      *([cache_kt] * pps), *([cache_v2] * pps))


def _expm1(y):
    acc = 1.0 + y * (1.0 / 15.0)
    for n in range(14, 1, -1):
        acc = 1.0 + (y * (1.0 / n)) * acc
    return jnp.where(jnp.abs(y) < 0.25, y * acc, jnp.exp(y) - 1.0)


def _log1p(z):
    return jnp.where(z < 1e-4, z * (1.0 - z * (0.5 - z * (1.0 / 3.0))), jnp.log(1.0 + z))


def _rg_gates(xc, wa_ref, ba_ref, wx_ref, bx_ref, lam_ref):
    xcb = xc.astype(BF16)
    r = _sigmoid(jnp.dot(xcb, wa_ref[...], preferred_element_type=F32) + ba_ref[...])
    ig = _sigmoid(jnp.dot(xcb, wx_ref[...], preferred_element_type=F32) + bx_ref[...])
    nl = -lam_ref[...]
    sp = jnp.maximum(nl, 0.0) + _log1p(jnp.exp(-jnp.abs(nl)))
    log_a = -RG_C * r * sp
    a = jnp.exp(log_a)
    bt = jnp.sqrt(-_expm1(2.0 * log_a)) * (ig * xc)
    return a, bt


def _rglru_seq_kernel(xr_ref, gr_ref, cw_ref, cb_ref, wa_ref, ba_ref, wx_ref, bx_ref, lam_ref,
                      rg_ref, ctail_ref, htail_ref, prev_scr, h_scr, *, ts):
    s = pl.program_id(1)

    @pl.when(s == 0)
    def _():
        prev_scr[...] = jnp.zeros(prev_scr.shape, F32)
        h_scr[...] = jnp.zeros(h_scr.shape, F32)

    x = xr_ref[...]
    prev = prev_scr[...]
    row8 = lax.broadcasted_iota(jnp.int32, (SUBLANES, 1), 0)
    cw = cw_ref[...]
    xc = cb_ref[...] + x * cw[CONV_W - 1:CONV_W]
    for jj in range(1, CONV_W):
        xs = pltpu.roll(x, jj, axis=0)
        ps = pltpu.roll(prev, jj, axis=0)
        top = jnp.where(row8 < jj, ps, xs[:SUBLANES])
        xs = jnp.concatenate([top, xs[SUBLANES:]], axis=0)
        xc = xc + xs * cw[CONV_W - 1 - jj:CONV_W - jj]
    a, bt = _rg_gates(xc, wa_ref, ba_ref, wx_ref, bx_ref, lam_ref)
    row = lax.broadcasted_iota(jnp.int32, (ts, 1), 0)
    d = 1
    while d < ts:
        a_s = pltpu.roll(a, d, axis=0)
        b_s = pltpu.roll(bt, d, axis=0)
        keep = row >= d
        bt = jnp.where(keep, a * b_s + bt, bt)
        a = jnp.where(keep, a * a_s, a)
        d *= 2
    h = a * h_scr[...] + bt
    rg_ref[...] = (h * jax.nn.gelu(gr_ref[...])).astype(rg_ref.dtype)
    h_scr[...] = h[ts - 1:ts]
    prev_scr[...] = x[ts - SUBLANES:ts]
    ctail_ref[...] = x[ts - SUBLANES:ts]
    htail_ref[...] = h[ts - SUBLANES:ts]


def _rglru_seq(xr, gr, cw, cb, wa_bd, ba, wx_bd, bx, lam, batch, seq):
    ts = min(256, seq)
    ns = seq // ts
    kern = functools.partial(_rglru_seq_kernel, ts=ts)
    cst = lambda a: pl.BlockSpec(a.shape, lambda b, s: (0,) * a.ndim)
    tile = pl.BlockSpec((ts, RG_W), lambda b, s: (b * ns + s, 0))
    tail = pl.BlockSpec((None, SUBLANES, RG_W), lambda b, s: (b, 0, 0))
    return pl.pallas_call(
        kern,
        grid=(batch, ns),
        in_specs=[tile, tile, cst(cw), cst(cb), cst(wa_bd), cst(ba), cst(wx_bd), cst(bx), cst(lam)],
        out_specs=(tile, tail, tail),
        out_shape=(jax.ShapeDtypeStruct((batch * seq, RG_W), BF16),
                   jax.ShapeDtypeStruct((batch, SUBLANES, RG_W), F32),
                   jax.ShapeDtypeStruct((batch, SUBLANES, RG_W), F32)),
        scratch_shapes=[pltpu.VMEM((SUBLANES, RG_W), F32), pltpu.VMEM((1, RG_W), F32)],
        compiler_params=_cparams(("parallel", "arbitrary")),
        name="rglru_seq",
    )(xr, gr, cw, cb, wa_bd, ba, wx_bd, bx, lam)


def _rglru_step_kernel(xr_ref, gr_ref, c0_ref, c1_ref, c2_ref, h0_ref, cw_ref, cb_ref, wa_ref, ba_ref,
                       wx_ref, bx_ref, lam_ref, rg_ref, h_ref):
    x = xr_ref[...]
    cw = cw_ref[...]
    xc = cb_ref[...] + c0_ref[...] * cw[0:1]
    xc = xc + c1_ref[...] * cw[1:2]
    xc = xc + c2_ref[...] * cw[2:3]
    xc = xc + x * cw[3:4]
    a, bt = _rg_gates(xc, wa_ref, ba_ref, wx_ref, bx_ref, lam_ref)
    h = a * h0_ref[...] + bt
    h_ref[...] = h
    rg_ref[...] = (h * jax.nn.gelu(gr_ref[...])).astype(rg_ref.dtype)


def _rglru_step(xr, gr, conv_state, h0, cw, cb, wa_bd, ba, wx_bd, bx, lam):
    nb = xr.shape[0]
    args = (xr, gr, conv_state[:, 0], conv_state[:, 1], conv_state[:, 2], h0,
            cw, cb, wa_bd, ba, wx_bd, bx, lam)
    return pl.pallas_call(
        _rglru_step_kernel,
        grid=(1,),
        in_specs=[_full(a.shape) for a in args],
        out_specs=(_full((nb, RG_W)), _full((nb, RG_W))),
        out_shape=(jax.ShapeDtypeStruct((nb, RG_W), BF16), jax.ShapeDtypeStruct((nb, RG_W), F32)),
        compiler_params=_cparams(("arbitrary",)),
        name="rglru_step",
    )(*args)


def _outproj_router_kernel(att_ref, rg_ref, x_ref, wo_ref, g2_ref, wr_ref, rb_ref, tri_ref,
                           x1_ref, hm8_ref, hmb_ref, idx_ref, gate_ref, rank_ref, cnt_ref,
                           carry_scr, *, tm, tiles_per_super):
    i = pl.program_id(0)

    @pl.when(i % tiles_per_super == 0)
    def _():
        carry_scr[...] = jnp.zeros(carry_scr.shape, F32)

    mixed = (jnp.dot(att_ref[...], wo_ref[:ATT_W, :], preferred_element_type=F32)
             + jnp.dot(rg_ref[...], wo_ref[ATT_W:, :], preferred_element_type=F32))
    x1 = x_ref[...] + mixed
    x1_ref[...] = x1
    ms = jnp.mean(x1 * x1, axis=-1, keepdims=True)
    hm = x1 * lax.rsqrt(ms + EPS) * g2_ref[...]
    hmb = hm.astype(BF16)
    hmb_ref[...] = hmb
    for s in range(SUBLANES):
        hm8_ref[pl.ds(s, tm, stride=SUBLANES), :] = hm[:, s * LANES:(s + 1) * LANES]

    scores = _sigmoid(_nt_dot(wr_ref[...], hmb))
    choice = scores + rb_ref[...]
    ch3 = choice.reshape(N_GROUPS, GROUP_SIZE, tm)
    i8 = lax.broadcasted_iota(jnp.int32, (N_GROUPS, GROUP_SIZE, tm), 1)
    m1 = jnp.max(ch3, axis=1, keepdims=True)
    f1 = jnp.min(jnp.where(ch3 == m1, i8, GROUP_SIZE), axis=1, keepdims=True)
    m2 = jnp.max(jnp.where(i8 == f1, -jnp.inf, ch3), axis=1, keepdims=True)
    gs3 = m1 + m2
    gs = gs3.reshape(N_GROUPS, tm)
    gi = lax.broadcasted_iota(jnp.int32, (N_GROUPS, tm), 0)
    grank = jnp.zeros((N_GROUPS, tm), jnp.int32)
    for g2 in range(N_GROUPS):
        rowv = gs[g2:g2 + 1, :]
        beats = (rowv > gs) | ((rowv == gs) & (g2 < gi))
        grank = grank + beats.astype(jnp.int32)
    gsel = (grank < TOPK_GROUPS).reshape(N_GROUPS, 1, tm)
    masked = jnp.where(gsel, ch3, NEG_INF).reshape(N_EXPERTS, tm)

    ei = lax.broadcasted_iota(jnp.int32, (N_EXPERTS, tm), 0)
    picks, gvals = [], []
    sel = jnp.zeros((N_EXPERTS, tm), F32)
    for _ in range(TOP_K):
        mx = jnp.max(masked, axis=0, keepdims=True)
        first = jnp.min(jnp.where(masked == mx, ei, N_EXPERTS), axis=0, keepdims=True)
        hit = ei == first
        picks.append(first)
        gvals.append(jnp.sum(jnp.where(hit, scores, 0.0), axis=0, keepdims=True))
        sel = jnp.where(hit, 1.0, sel)
        masked = jnp.where(hit, -jnp.inf, masked)
    gsum = gvals[0]
    for kk in range(1, TOP_K):
        gsum = gsum + gvals[kk]

    pos = jnp.dot(sel.astype(BF16), tri_ref[...], preferred_element_type=F32) + carry_scr[...]
    for kk in range(TOP_K):
        idx_ref[kk:kk + 1, :] = picks[kk]
        gate_ref[kk:kk + 1, :] = gvals[kk] / gsum * ROUTE_SCALE
        rk = jnp.sum(jnp.where(ei == picks[kk], pos, 0.0), axis=0, keepdims=True)
        rank_ref[kk:kk + 1, :] = rk.astype(jnp.int32)
    carry = carry_scr[...] + jnp.sum(sel, axis=1, keepdims=True)
    carry_scr[...] = carry
    cnt_ref[...] = jnp.broadcast_to(carry, cnt_ref.shape)


def _outproj_router(att, rg, x2, wo_bf, g2, wr_t, rb_col, tt):
    T, D = x2.shape
    tm = min(256, T)
    tiles_per_super = tt // tm
    n_super = T // tt
    tri = jnp.triu(jnp.ones((tm, tm), BF16), k=1)
    kern = functools.partial(_outproj_router_kernel, tm=tm, tiles_per_super=tiles_per_super)
    row = lambda w: pl.BlockSpec((tm, w), lambda i: (i, 0))
    col = pl.BlockSpec((TOP_K, tm), lambda i: (0, i))
    return pl.pallas_call(
        kern,
        grid=(T // tm,),
        in_specs=[row(ATT_W), row(RG_W), row(D), _full(wo_bf.shape), _full(g2.shape),
                  _full(wr_t.shape), _full(rb_col.shape), _full(tri.shape)],
        out_specs=(row(D), pl.BlockSpec((tm * SUBLANES, LANES), lambda i: (i, 0)), row(D),
                   col, col, col,
                   pl.BlockSpec((None, N_EXPERTS, LANES), lambda i: (i // tiles_per_super, 0, 0))),
        out_shape=(jax.ShapeDtypeStruct((T, D), F32),
                   jax.ShapeDtypeStruct((T * SUBLANES, LANES), F32),
                   jax.ShapeDtypeStruct((T, D), BF16),
                   jax.ShapeDtypeStruct((TOP_K, T), jnp.int32),
                   jax.ShapeDtypeStruct((TOP_K, T), F32),
                   jax.ShapeDtypeStruct((TOP_K, T), jnp.int32),
                   jax.ShapeDtypeStruct((n_super, N_EXPERTS, LANES), F32)),
        scratch_shapes=[pltpu.VMEM((N_EXPERTS, 1), F32)],
        compiler_params=_cparams(("arbitrary",)),
        name="outproj_router",
    )(att, rg, x2, wo_bf, g2, wr_t, rb_col, tri)


def _moe_kernel(cnt_ref, off_ref, hm8_ref, w13_ref, w2_ref, tok_hbm, gate_hbm, y8_ref,
                tok_s, gate_s, xg, og, sem, *, tt, mch, d_model, d_exp):
    ti = pl.program_id(0)
    e = pl.program_id(1)
    n_list = tt * TOP_K
    chunks = d_model // LANES

    @pl.when(e == 0)
    def _():
        cp_t = pltpu.make_async_copy(tok_hbm.at[pl.ds(ti * n_list, n_list)], tok_s, sem.at[0])
        cp_g = pltpu.make_async_copy(gate_hbm.at[pl.ds(ti * n_list, n_list)], gate_s, sem.at[1])
        cp_t.start()
        cp_g.start()
        y8_ref[...] = jnp.zeros(y8_ref.shape, F32)
        cp_t.wait()
        cp_g.wait()

    c = cnt_ref[ti * N_EXPERTS + e]
    off = off_ref[ti * N_EXPERTS + e]
    nch = (c + mch - 1) // mch
    unroll = 8

    def chunk(ch, carry):
        base = off + ch * mch
        nvalid = c - ch * mch

        def gather(io, _):
            for u in range(unroll):
                ii = io * unroll + u
                pos = jnp.minimum(base + ii, n_list - 1)
                t = tok_s[pos]
                src = pl.multiple_of(t * chunks, chunks)
                dst = pl.multiple_of(ii * chunks, chunks)
                xg[pl.ds(dst, chunks), :] = hm8_ref[pl.ds(src, chunks), :]
            return 0

        lax.fori_loop(0, mch // unroll, gather, 0)
        xb = jnp.concatenate([xg[pl.ds(s, mch, stride=chunks), :] for s in range(chunks)],
                             axis=-1).astype(BF16)
        h13 = jnp.dot(xb, w13_ref[...], preferred_element_type=F32)
        h1 = h13[:, :d_exp]
        hdn = (h1 * _sigmoid(h1)) * h13[:, d_exp:]
        o = jnp.dot(hdn.astype(BF16), w2_ref[...], preferred_element_type=F32)
        for s in range(chunks):
            og[pl.ds(s, mch, stride=chunks), :] = o[:, s * LANES:(s + 1) * LANES]

        def scatter(io, _):
            for u in range(unroll):
                ii = io * unroll + u
                pos = jnp.minimum(base + ii, n_list - 1)
                t = tok_s[pos]
                gt = jnp.where(ii < nvalid, gate_s[pos], 0.0)
                dst = pl.multiple_of(t * chunks, chunks)
                src = pl.multiple_of(ii * chunks, chunks)
                y8_ref[pl.ds(dst, chunks), :] = (y8_ref[pl.ds(dst, chunks), :]
                                                 + gt * og[pl.ds(src, chunks), :])
            return 0

        lax.fori_loop(0, mch // unroll, scatter, 0)
        return carry

    lax.fori_loop(0, nch, chunk, 0)


def _moe(hm8, w13, w2, tok_sorted, gate_sorted, cnt_flat, off_flat, tt, mch):
    T8, _ = hm8.shape
    T = T8 // SUBLANES
    n_tiles = T // tt
    d_model = w13.shape[1]
    d_exp = w2.shape[1]
    assert d_model == SUBLANES * LANES
    kern = functools.partial(_moe_kernel, tt=tt, mch=mch, d_model=d_model, d_exp=d_exp)
    grid_spec = pltpu.PrefetchScalarGridSpec(
        num_scalar_prefetch=2,
        grid=(n_tiles, N_EXPERTS),
        in_specs=[pl.BlockSpec((tt * SUBLANES, LANES), lambda t, e, c, o: (t, 0)),
                  pl.BlockSpec((None, d_model, 2 * d_exp), lambda t, e, c, o: (e, 0, 0)),
                  pl.BlockSpec((None, d_exp, d_model), lambda t, e, c, o: (e, 0, 0)),
                  pl.BlockSpec(memory_space=pl.ANY),
                  pl.BlockSpec(memory_space=pl.ANY)],
        out_specs=pl.BlockSpec((tt * SUBLANES, LANES), lambda t, e, c, o: (t, 0)),
        scratch_shapes=[pltpu.SMEM((tt * TOP_K,), jnp.int32), pltpu.SMEM((tt * TOP_K,), F32),
                        pltpu.VMEM((mch * SUBLANES, LANES), F32),
                        pltpu.VMEM((mch * SUBLANES, LANES), F32),
                        pltpu.SemaphoreType.DMA((2,))],
    )
    return pl.pallas_call(
        kern,
        grid_spec=grid_spec,
        out_shape=jax.ShapeDtypeStruct((T8, LANES), F32),
        compiler_params=_cparams(("arbitrary", "arbitrary")),
        name="moe_experts",
    )(cnt_flat, off_flat, hm8, w13, w2, tok_sorted, gate_sorted)


def _final_kernel(y8_ref, x1_ref, hmb_ref, ws13_ref, ws2_ref, gp_ref, wg_ref, p_ref, wp_ref, o_ref,
                  *, tm, d_sh):
    chunks = x1_ref.shape[1] // LANES
    routed = jnp.concatenate([y8_ref[pl.ds(s, tm, stride=chunks), :] for s in range(chunks)], axis=-1)
    hmb = hmb_ref[...]
    h13 = jnp.dot(hmb, ws13_ref[...], preferred_element_type=F32)
    h1 = h13[:, :d_sh]
    hdn = (h1 * _sigmoid(h1)) * h13[:, d_sh:]
    shared = jnp.dot(hdn.astype(BF16), ws2_ref[...], preferred_element_type=F32)
    x2 = x1_ref[...] + (routed + shared)
    ms = jnp.mean(x2 * x2, axis=-1, keepdims=True)
    hp = (x2 * lax.rsqrt(ms + EPS) * gp_ref[...]).astype(BF16)
    gate = _sigmoid(jnp.dot(hp, wg_ref[...], preferred_element_type=F32))
    proj = jnp.dot(p_ref[...].astype(BF16), wp_ref[...], preferred_element_type=F32)
    o_ref[...] = x2 + gate * proj


def _final(y8, x1, hmb, ws13, ws2, gp, wg, p2, wp):
    T, D = x1.shape
    tm = min(256, T)
    d_sh = ws2.shape[0]
    kern = functools.partial(_final_kernel, tm=tm, d_sh=d_sh)
    row = lambda w: pl.BlockSpec((tm, w), lambda i: (i, 0))
    return pl.pallas_call(
        kern,
        grid=(T // tm,),
        in_specs=[pl.BlockSpec((tm * SUBLANES, LANES), lambda i: (i, 0)), row(D), row(D),
                  _full(ws13.shape), _full(ws2.shape), _full(gp.shape), _full(wg.shape),
                  row(p2.shape[1]), _full(wp.shape)],
        out_specs=row(D),
        out_shape=jax.ShapeDtypeStruct((T, D), F32),
        compiler_params=_cparams(("parallel",)),
        name="final",
    )(y8, x1, hmb, ws13, ws2, gp, wg, p2, wp)


def _block_diag(w):
    nh, d, _ = w.shape
    eye = jnp.eye(nh, dtype=w.dtype)
    return (eye[:, None, :, None] * w[:, :, None, :]).reshape(nh * d, nh * d)


def _prep_weights(i, g_norm1, w_in, g_q, g_k, lam_q1, lam_k1, lam_q2, lam_k2, g_subln, conv_w, conv_b,
                  w_a, b_a, w_x, b_x, lam_rg, w_out, g_norm2, w_router, router_bias, w_e1, w_e3, w_e2,
                  w_s1, w_s3, w_s2, g_ple, w_ple_gate, w_ple_proj):
    reps = ATT_W // HEAD_DIM
    seg = jnp.arange(ATT_W) // HEAD_DIM
    bd = ((seg[:, None] == seg[None, :]).astype(F32) / HEAD_DIM).astype(BF16)
    return dict(
        g1=g_norm1[i][None], w_in=w_in[i].astype(BF16),
        gq=jnp.tile(g_q[i], reps)[None], gk=jnp.tile(g_k[i], reps)[None], bd=bd,
        lq1=lam_q1[i][None], lk1=lam_k1[i][None], lq2=lam_q2[i][None], lk2=lam_k2[i][None],
        gs=g_subln[i][None],
        cw=conv_w[i], cb=conv_b[i][None],
        wa=_block_diag(w_a[i]).astype(BF16), ba=b_a[i][None],
        wx=_block_diag(w_x[i]).astype(BF16), bx=b_x[i][None], lam_rg=lam_rg[i][None],
        wo=w_out[i].astype(BF16), g2=g_norm2[i][None],
        wr_t=w_router[i].T.astype(BF16), rb=router_bias[i][:, None],
        w13=jnp.concatenate([w_e1[i], w_e3[i]], axis=-1).astype(BF16), w2=w_e2[i].astype(BF16),
        ws13=jnp.concatenate([w_s1[i], w_s3[i]], axis=-1).astype(BF16), ws2=w_s2[i].astype(BF16),
        gp=g_ple[i][None], wg=w_ple_gate[i].astype(BF16), wp=w_ple_proj[i].astype(BF16),
    )


def _moe_stage(att, rg, x2, p2, W, tt, mch):
    T = x2.shape[0]
    x1, hm8, hmb, idx, gate, rank, cnt = _outproj_router(att, rg, x2, W['wo'], W['g2'], W['wr_t'],
                                                         W['rb'], tt)
    n_tiles = T // tt
    cnt_i = cnt[:, :, 0].astype(jnp.int32)
    off_i = jnp.cumsum(cnt_i, axis=1) - cnt_i
    tile_of = (jnp.arange(T, dtype=jnp.int32) // tt)[None, :]
    tok_local = (jnp.arange(T, dtype=jnp.int32) % tt)[None, :]
    dest = tile_of * (tt * TOP_K) + off_i[tile_of, idx] + rank
    tok_sorted = jnp.zeros((T * TOP_K,), jnp.int32).at[dest.reshape(-1)].set(
        jnp.broadcast_to(tok_local, dest.shape).reshape(-1))
    gate_sorted = jnp.zeros((T * TOP_K,), F32).at[dest.reshape(-1)].set(gate.reshape(-1))
    y8 = _moe(hm8, W['w13'], W['w2'], tok_sorted, gate_sorted, cnt_i.reshape(-1), off_i.reshape(-1),
              tt, mch)
    return _final(y8, x1, hmb, W['ws13'], W['ws2'], W['gp'], W['wg'], p2, W['wp'])


def _slopes_col():
    h = jnp.arange(N_MAPS) // 2 + 1
    return jnp.exp2(-8.0 * h.astype(F32) / N_HEADS)[:, None]


def kernel(x_prompt, x_sample, cache_k, cache_v, state_conv, state_h, page_table, p_prompt, p_sample, g_norm1, w_in, g_q, g_k, lam_q1, lam_k1, lam_q2, lam_k2, g_subln, conv_w, conv_b, w_a, b_a, w_x, b_x, lam_rg, w_out, g_norm2, w_router, router_bias, w_e1, w_e3, w_e2, w_s1, w_s3, w_s2, g_ple, w_ple_gate, w_ple_proj):
    depth = w_in.shape[0]
    B, S, D = x_prompt.shape
    NB, NS, _ = x_sample.shape
    assert NS == 1
    n_pool = cache_k.shape[1]
    yp = x_prompt.reshape(B * S, D)
    ys = x_sample.reshape(NB * NS, D)
    kp_l, vp_l, cp_l, hp_l, ks_l, vs_l, cs_l, hs_l = [], [], [], [], [], [], [], []
    for i in range(depth):
        W = _prep_weights(i, g_norm1, w_in, g_q, g_k, lam_q1, lam_k1, lam_q2, lam_k2, g_subln, conv_w,
                          conv_b, w_a, b_a, w_x, b_x, lam_rg, w_out, g_norm2, w_router, router_bias,
                          w_e1, w_e3, w_e2, w_s1, w_s3, w_s2, g_ple, w_ple_gate, w_ple_proj)
        lam_init = 0.8 - 0.6 * math.exp(-0.3 * i)
        lam_args = (W['lq1'], W['lk1'], W['lq2'], W['lk2'], W['gs'])

        qe, k, kb, v, vb, xr, gr = _inproj(yp, W['g1'], W['w_in'], W['gq'], W['gk'], W['bd'])
        att = _prompt_attention(qe, kb, vb, *lam_args, B, S, lam_init)
        rg, ctail, htail = _rglru_seq(xr, gr, W['cw'], W['cb'], W['wa'], W['ba'], W['wx'], W['bx'],
                                      W['lam_rg'], B, S)
        tt_p = min(2048, B * S)
        yp = _moe_stage(att, rg, yp, p_prompt[i].reshape(B * S, -1), W, tt_p, min(256, tt_p))
        kp_l.append(k.reshape(B, S, N_HEADS, 2, HEAD_DIM))
        vp_l.append(v.reshape(B, S, N_HEADS, V_DIM))
        cp_l.append(ctail[:, SUBLANES - (CONV_W - 1):])
        hp_l.append(htail[:, SUBLANES - 1])

        qe, k, kb, v, vb, xr, gr = _inproj(ys, W['g1'], W['w_in'], W['gq'], W['gk'], W['bd'])
        qh = qe.reshape(NB, N_MAPS, V_DIM)
        hsel = (jnp.arange(N_MAPS)[:, None] // 2 == jnp.arange(N_HEADS)[None, :])
        qblk = (qh[:, :, None, :] * hsel[None, :, :, None].astype(BF16)).reshape(NB, N_MAPS, ATT_W)
        v_rows = jnp.repeat(v.reshape(NB, N_HEADS, V_DIM), 2, axis=1)
        cache_kt = jnp.transpose(cache_k[i], (0, 2, 3, 4, 1)).reshape(n_pool, ATT_W, PAGE)
        cache_v2 = cache_v[i].reshape(n_pool, PAGE * N_HEADS, V_DIM)
        att_s = _decode_attention(qblk, k.reshape(NB, 1, ATT_W), v_rows, _slopes_col(), *lam_args,
                                  cache_kt, cache_v2, page_table, lam_init)
        rg_s, h_s = _rglru_step(xr, gr, state_conv[i], state_h[i], W['cw'], W['cb'], W['wa'], W['ba'],
                                W['wx'], W['bx'], W['lam_rg'])
        ys = _moe_stage(att_s.reshape(NB, ATT_W).astype(BF16), rg_s, ys,
                        p_sample[i].reshape(NB * NS, -1), W, NB, 32)
        ks_l.append(k.reshape(NB, NS, N_HEADS, 2, HEAD_DIM))
        vs_l.append(v.reshape(NB, NS, N_HEADS, V_DIM))
        cs_l.append(jnp.concatenate([state_conv[i][:, 1:], xr[:, None, :]], axis=1))
        hs_l.append(h_s)
    return (yp.reshape(B, S, D), ys.reshape(NB, NS, D), jnp.stack(kp_l), jnp.stack(vp_l),
            jnp.stack(cp_l), jnp.stack(hp_l), jnp.stack(ks_l), jnp.stack(vs_l), jnp.stack(cs_l),
            jnp.stack(hs_l))
```

```python
import functools
import math

import jax
import jax.numpy as jnp
from jax import lax
from jax.experimental import pallas as pl
from jax.experimental.pallas import tpu as pltpu

F32 = jnp.float32
BF16 = jnp.bfloat16

EPS = 1e-6
NEG_INF = -1e30
LANES = 128
SUBLANES = 8
N_HEADS = 4
HEAD_DIM = 64
V_DIM = 2 * HEAD_DIM
N_MAPS = 2 * N_HEADS
ATT_W = N_HEADS * V_DIM
RG_W = 512
RG_C = 8.0
CONV_W = 4
N_EXPERTS = 64
N_GROUPS = 8
GROUP_SIZE = N_EXPERTS // N_GROUPS
TOPK_GROUPS = 4
TOP_K = 8
ROUTE_SCALE = 2.5
ATT_SCALE = HEAD_DIM ** -0.5
LOG2E = math.log2(math.e)
VE_ROWS = V_DIM + 16
PAGE = 128
VMEM_LIMIT = 56 * 1024 * 1024


def _cparams(sem):
    return pltpu.CompilerParams(dimension_semantics=sem, vmem_limit_bytes=VMEM_LIMIT)


def _sigmoid(x):
    return 1.0 / (1.0 + jnp.exp(-x))


def _nt_dot(a, b):
    return lax.dot_general(a, b, (((1,), (1,)), ((), ())), preferred_element_type=F32)


def _full(shape):
    n = len(shape)
    return pl.BlockSpec(shape, lambda *_: (0,) * n)


def _inproj_kernel(x_ref, g1_ref, w_ref, wqt_ref, wvt_ref, gq_ref, gk_ref, bd_ref,
                   qt_ref, k_ref, kb_ref, v_ref, vte_ref, xr_ref, gr_ref):
    x = x_ref[...]
    tm = x.shape[0]
    ms = jnp.mean(x * x, axis=-1, keepdims=True)
    h = (x * lax.rsqrt(ms + EPS) * g1_ref[...]).astype(BF16)

    def proj(lo, hi):
        return jnp.dot(h, w_ref[:, lo:hi], preferred_element_type=F32)

    qt = _nt_dot(wqt_ref[...], h).reshape(N_MAPS, HEAD_DIM, tm)
    qms = jnp.mean(qt * qt, axis=1, keepdims=True)
    qn = (qt * lax.rsqrt(qms + EPS) * gq_ref[...][None]) * (ATT_SCALE * LOG2E)
    zeros = jnp.zeros((HEAD_DIM, tm), BF16)
    for mm in range(N_MAPS):
        c = mm % 2
        qt_ref[mm * V_DIM + c * HEAD_DIM:mm * V_DIM + (c + 1) * HEAD_DIM, :] = qn[mm].astype(BF16)
        qt_ref[mm * V_DIM + (1 - c) * HEAD_DIM:mm * V_DIM + (2 - c) * HEAD_DIM, :] = zeros

    def segnorm(z, g):
        sq = z * z
        hi = sq.astype(BF16)
        lo = (sq - hi.astype(F32)).astype(BF16)
        m = (jnp.dot(hi, bd_ref[...], preferred_element_type=F32)
             + jnp.dot(lo, bd_ref[...], preferred_element_type=F32))
        return z * lax.rsqrt(m + EPS) * g

    k = segnorm(proj(ATT_W, 2 * ATT_W), gk_ref[...])
    k_ref[...] = k
    kb_ref[...] = k.astype(BF16)
    v_ref[...] = proj(2 * ATT_W, 3 * ATT_W)
    vt = _nt_dot(wvt_ref[...], h).astype(BF16)
    ones = jnp.ones((VE_ROWS - V_DIM, tm), BF16)
    for hh in range(N_HEADS):
        vte_ref[hh * VE_ROWS:hh * VE_ROWS + V_DIM, :] = vt[hh * V_DIM:(hh + 1) * V_DIM]
        vte_ref[hh * VE_ROWS + V_DIM:(hh + 1) * VE_ROWS, :] = ones
    xr_ref[...] = proj(3 * ATT_W, 3 * ATT_W + RG_W)
    gr_ref[...] = proj(3 * ATT_W + RG_W, 3 * ATT_W + 2 * RG_W)


def _inproj(x2, g1, w_in_bf, wqt, wvt, gq_col, gk_t, bd):
    T, D = x2.shape
    tm = min(256, T)
    row = lambda w: pl.BlockSpec((tm, w), lambda i: (i, 0))
    col = lambda r: pl.BlockSpec((r, tm), lambda i: (0, i))
    out_shape = (
        jax.ShapeDtypeStruct((N_MAPS * V_DIM, T), BF16),
        jax.ShapeDtypeStruct((T, ATT_W), F32), jax.ShapeDtypeStruct((T, ATT_W), BF16),
        jax.ShapeDtypeStruct((T, ATT_W), F32), jax.ShapeDtypeStruct((N_HEADS * VE_ROWS, T), BF16),
        jax.ShapeDtypeStruct((T, RG_W), F32), jax.ShapeDtypeStruct((T, RG_W), F32),
    )
    args = (x2, g1, w_in_bf, wqt, wvt, gq_col, gk_t, bd)
    return pl.pallas_call(
        _inproj_kernel,
        grid=(T // tm,),
        in_specs=[row(D)] + [_full(a.shape) for a in args[1:]],
        out_specs=(col(N_MAPS * V_DIM), row(ATT_W), row(ATT_W), row(ATT_W), col(N_HEADS * VE_ROWS),
                   row(RG_W), row(RG_W)),
        out_shape=out_shape,
        compiler_params=_cparams(("parallel",)),
        name="inproj",
    )(*args)


def _lam_value(lq1, lk1, lq2, lk2, lam_init):
    s1 = jnp.sum(lq1 * lk1, axis=-1, keepdims=True)
    s2 = jnp.sum(lq2 * lk2, axis=-1, keepdims=True)
    return jnp.exp(s1) - jnp.exp(s2) + lam_init


def _subln(o, g, lam_init):
    ms = jnp.mean(o * o, axis=-1, keepdims=True)
    return (o * lax.rsqrt(ms + EPS) * g) * (1.0 - lam_init)


def _prompt_attn_kernel(it_ref, jt_ref, qt_ref, kb_ref, vte_ref, lq1_ref, lk1_ref, lq2_ref, lk2_ref,
                        gsc_ref, o_ref, m_scr, acc_scr, bias_scr, *, tb, lam_init):
    pidx = pl.program_id(1)
    i = it_ref[pidx]
    j = jt_ref[pidx]
    slopes2 = [LOG2E * 2.0 ** (-8.0 * (hh + 1) / N_HEADS) for hh in range(N_HEADS)]

    @pl.when(pidx == 0)
    def _():
        kl = lax.broadcasted_iota(jnp.int32, (tb, tb), 0)
        ql = lax.broadcasted_iota(jnp.int32, (tb, tb), 1)
        for hh in range(N_HEADS):
            cb = slopes2[hh] * kl.astype(F32)
            bias_scr[hh] = cb
            bias_scr[N_HEADS + hh] = jnp.where(kl <= ql, cb, NEG_INF)

    @pl.when(j == 0)
    def _():
        m_scr[...] = jnp.full(m_scr.shape, NEG_INF, F32)
        acc_scr[...] = jnp.zeros(acc_scr.shape, F32)

    diag = (j == i).astype(jnp.int32)
    gap = ((i - j) * tb).astype(F32)
    for hh in range(N_HEADS):
        off = -slopes2[hh] * gap
        bias = bias_scr[diag * N_HEADS + hh]
        kh = kb_ref[:, hh * V_DIM:(hh + 1) * V_DIM]
        vte = vte_ref[hh * VE_ROWS:(hh + 1) * VE_ROWS, :]
        for c in range(2):
            mm = 2 * hh + c
            u = jnp.dot(kh, qt_ref[mm * V_DIM:(mm + 1) * V_DIM, :], preferred_element_type=F32) + bias
            m_prev = m_scr[mm]
            m_new = jnp.maximum(m_prev, jnp.max(u, axis=0, keepdims=True) + off)
            alpha = jnp.exp2(m_prev - m_new)
            p = jnp.exp2(u - (m_new - off))
            acc_scr[mm] = alpha * acc_scr[mm] + jnp.dot(vte, p.astype(BF16),
                                                         preferred_element_type=F32)
            m_scr[mm] = m_new

    @pl.when(j == i)
    def _():
        lam = _lam_value(lq1_ref[...], lk1_ref[...], lq2_ref[...], lk2_ref[...], lam_init)
        for hh in range(N_HEADS):
            a1 = acc_scr[2 * hh]
            a2 = acc_scr[2 * hh + 1]
            o1 = a1[:V_DIM] / a1[V_DIM:V_DIM + 1]
            o2 = a2[:V_DIM] / a2[V_DIM:V_DIM + 1]
            o = o1 - lam * o2
            ms = jnp.mean(o * o, axis=0, keepdims=True)
            y = (o * lax.rsqrt(ms + EPS) * gsc_ref[...]) * (1.0 - lam_init)
            o_ref[:, hh * V_DIM:(hh + 1) * V_DIM] = y.T.astype(o_ref.dtype)


def _prompt_attention(qt, kb, vte, lq1, lk1, lq2, lk2, gs_col, batch, seq, lam_init):
    tb = min(512, seq)
    nb = seq // tb
    pairs = [(i, j) for i in range(nb) for j in range(i + 1)]
    i_tab = jnp.asarray([p[0] for p in pairs], jnp.int32)
    j_tab = jnp.asarray([p[1] for p in pairs], jnp.int32)
    cst = lambda a: pl.BlockSpec(a.shape, lambda b, p, it, jt: (0,) * a.ndim)
    kern = functools.partial(_prompt_attn_kernel, tb=tb, lam_init=lam_init)
    grid_spec = pltpu.PrefetchScalarGridSpec(
        num_scalar_prefetch=2,
        grid=(batch, len(pairs)),
        in_specs=[pl.BlockSpec((N_MAPS * V_DIM, tb), lambda b, p, it, jt: (0, b * nb + it[p])),
                  pl.BlockSpec((tb, ATT_W), lambda b, p, it, jt: (b * nb + jt[p], 0)),
                  pl.BlockSpec((N_HEADS * VE_ROWS, tb), lambda b, p, it, jt: (0, b * nb + jt[p])),
                  cst(lq1), cst(lk1), cst(lq2), cst(lk2), cst(gs_col)],
        out_specs=pl.BlockSpec((tb, ATT_W), lambda b, p, it, jt: (b * nb + it[p], 0)),
        scratch_shapes=[pltpu.VMEM((N_MAPS, 1, tb), F32),
                        pltpu.VMEM((N_MAPS, VE_ROWS, tb), F32),
                        pltpu.VMEM((2 * N_HEADS, tb, tb), F32)],
    )
    return pl.pallas_call(
        kern,
        grid_spec=grid_spec,
        out_shape=jax.ShapeDtypeStruct((batch * seq, ATT_W), BF16),
        compiler_params=_cparams(("arbitrary", "arbitrary")),
        name="prompt_attn",
    )(i_tab, j_tab, qt, kb, vte, lq1, lk1, lq2, lk2, gs_col)


def _decode_attn_kernel(pt_ref, qb_ref, kn_ref, vn_ref, sl_ref, lq1_ref, lk1_ref, lq2_ref, lk2_ref,
                        gs_ref, *refs, pps, past, lam_init):
    kp_refs = refs[:pps]
    vp_refs = refs[pps:2 * pps]
    o_ref, m_scr, l_scr, acc_scr = refs[2 * pps:]
    g = pl.program_id(1)
    ng = pl.num_programs(1)

    @pl.when(g == 0)
    def _():
        m_scr[...] = jnp.full(m_scr.shape, NEG_INF, F32)
        l_scr[...] = jnp.zeros(l_scr.shape, F32)
        acc_scr[...] = jnp.zeros(acc_scr.shape, F32)

    q = qb_ref[...]
    slopes = sl_ref[...]
    lane = lax.broadcasted_iota(jnp.int32, (1, PAGE), 1)
    head_of_row = lax.broadcasted_iota(jnp.int32, (N_MAPS, 1), 0) // 2
    m_run = m_scr[...]
    l_run = l_scr[...]
    acc = acc_scr[...]
    scores = []
    for pi in range(pps):
        kpos = (g * pps + pi) * PAGE + lane
        dist = (past - kpos).astype(F32)
        s = jnp.dot(q, kp_refs[pi][...].astype(BF16), preferred_element_type=F32)
        scores.append(s - slopes * dist)
    smax = scores[0]
    for s in scores[1:]:
        smax = jnp.maximum(smax, s)
    m_new = jnp.maximum(m_run, jnp.max(smax, axis=-1, keepdims=True))
    alpha = jnp.exp2(m_run - m_new)
    psum = jnp.zeros((N_MAPS, PAGE), F32)
    pvs = [jnp.zeros((N_MAPS, V_DIM), F32) for _ in range(N_HEADS)]
    for pi in range(pps):
        p = jnp.exp2(scores[pi] - m_new)
        psum = psum + p
        pb = p.astype(BF16)
        for hh in range(N_HEADS):
            vh = vp_refs[pi][pl.ds(hh, PAGE, stride=N_HEADS), :].astype(BF16)
            pvs[hh] = pvs[hh] + jnp.dot(pb, vh, preferred_element_type=F32)
    pv = pvs[0]
    for hh in range(1, N_HEADS):
        pv = jnp.where(head_of_row == hh, pvs[hh], pv)
    m_scr[...] = m_new
    l_run = alpha * l_run + jnp.sum(psum, axis=-1, keepdims=True)
    acc = alpha * acc + pv
    l_scr[...] = l_run
    acc_scr[...] = acc
    m_run = m_new

    @pl.when(g == ng - 1)
    def _():
        kn = kn_ref[...].astype(BF16).astype(F32)
        vn = vn_ref[...].astype(BF16).astype(F32)
        s = jnp.sum(q.astype(F32) * kn, axis=-1, keepdims=True)
        m_new = jnp.maximum(m_run, s)
        alpha = jnp.exp2(m_run - m_new)
        p = jnp.exp2(s - m_new)
        l_fin = alpha * l_run + p
        acc_fin = alpha * acc + p.astype(BF16).astype(F32) * vn
        lam = _lam_value(lq1_ref[...], lk1_ref[...], lq2_ref[...], lk2_ref[...], lam_init)
        outn = acc_fin / l_fin
        for hh in range(N_HEADS):
            o = outn[2 * hh:2 * hh + 1] - lam * outn[2 * hh + 1:2 * hh + 2]
            o_ref[:, hh * V_DIM:(hh + 1) * V_DIM] = _subln(o, gs_ref[...], lam_init)


def _decode_attention(qblk, k_new, v_rows, slopes, lq1, lk1, lq2, lk2, gs, cache_kt, cache_v2,
                      page_table, lam_init):
    nb, n_pages = page_table.shape
    pps = 8
    while n_pages % pps:
        pps //= 2
    past = n_pages * PAGE
    pt_flat = page_table.reshape(-1)

    def page_map(pi):
        return lambda b, g, pt: (pt[b * n_pages + g * pps + pi], 0, 0)

    k_specs = [pl.BlockSpec((None, ATT_W, PAGE), page_map(pi)) for pi in range(pps)]
    v_specs = [pl.BlockSpec((None, PAGE * N_HEADS, V_DIM), page_map(pi)) for pi in range(pps)]
    cst = lambda shape: pl.BlockSpec(shape, lambda b, g, pt: (0,) * len(shape))
    kern = functools.partial(_decode_attn_kernel, pps=pps, past=past, lam_init=lam_init)
    grid_spec = pltpu.PrefetchScalarGridSpec(
        num_scalar_prefetch=1,
        grid=(nb, n_pages // pps),
        in_specs=[pl.BlockSpec((None, N_MAPS, ATT_W), lambda b, g, pt: (b, 0, 0)),
                  pl.BlockSpec((None, 1, ATT_W), lambda b, g, pt: (b, 0, 0)),
                  pl.BlockSpec((None, N_MAPS, V_DIM), lambda b, g, pt: (b, 0, 0)),
                  cst(slopes.shape), cst(lq1.shape), cst(lk1.shape), cst(lq2.shape), cst(lk2.shape),
                  cst(gs.shape)] + k_specs + v_specs,
        out_specs=pl.BlockSpec((None, 1, ATT_W), lambda b, g, pt: (b, 0, 0)),
        scratch_shapes=[pltpu.VMEM((N_MAPS, 1), F32), pltpu.VMEM((N_MAPS, 1), F32),
                        pltpu.VMEM((N_MAPS, V_DIM), F32)],
    )
    return pl.pallas_call(
        kern,
        grid_spec=grid_spec,
        out_shape=jax.ShapeDtypeStruct((nb, 1, ATT_W), F32),
        compiler_params=_cparams(("parallel", "arbitrary")),
        name="decode_attn",
    )(pt_flat, qblk, k_new, v_rows, slopes, lq1, lk1, lq2, lk2, gs,
      *([cache_kt] * pps), *([cache_v2] * pps))


def _expm1(y):
    acc = 1.0 + y * (1.0 / 15.0)
    for n in range(14, 1, -1):
        acc = 1.0 + (y * (1.0 / n)) * acc
    return jnp.where(jnp.abs(y) < 0.25, y * acc, jnp.exp(y) - 1.0)


def _log1p(z):
    return jnp.where(z < 1e-4, z * (1.0 - z * (0.5 - z * (1.0 / 3.0))), jnp.log(1.0 + z))


def _rg_gates(xc, wa_ref, ba_ref, wx_ref, bx_ref, lam_ref):
    xcb = xc.astype(BF16)
    r = _sigmoid(jnp.dot(xcb, wa_ref[...], preferred_element_type=F32) + ba_ref[...])
    ig = _sigmoid(jnp.dot(xcb, wx_ref[...], preferred_element_type=F32) + bx_ref[...])
    nl = -lam_ref[...]
    sp = jnp.maximum(nl, 0.0) + _log1p(jnp.exp(-jnp.abs(nl)))
    log_a = -RG_C * r * sp
    a = jnp.exp(log_a)
    bt = jnp.sqrt(-_expm1(2.0 * log_a)) * (ig * xc)
    return a, bt


def _rglru_seq_kernel(xr_ref, gr_ref, cw_ref, cb_ref, wa_ref, ba_ref, wx_ref, bx_ref, lam_ref,
                      rg_ref, ctail_ref, htail_ref, prev_scr, h_scr, *, ts):
    s = pl.program_id(1)

    @pl.when(s == 0)
    def _():
        prev_scr[...] = jnp.zeros(prev_scr.shape, F32)
        h_scr[...] = jnp.zeros(h_scr.shape, F32)

    x = xr_ref[...]
    prev = prev_scr[...]
    row8 = lax.broadcasted_iota(jnp.int32, (SUBLANES, 1), 0)
    cw = cw_ref[...]
    xc = cb_ref[...] + x * cw[CONV_W - 1:CONV_W]
    for jj in range(1, CONV_W):
        xs = pltpu.roll(x, jj, axis=0)
        ps = pltpu.roll(prev, jj, axis=0)
        top = jnp.where(row8 < jj, ps, xs[:SUBLANES])
        xs = jnp.concatenate([top, xs[SUBLANES:]], axis=0)
        xc = xc + xs * cw[CONV_W - 1 - jj:CONV_W - jj]
    a, bt = _rg_gates(xc, wa_ref, ba_ref, wx_ref, bx_ref, lam_ref)
    row = lax.broadcasted_iota(jnp.int32, (ts, 1), 0)
    d = 1
    while d < ts:
        a_s = pltpu.roll(a, d, axis=0)
        b_s = pltpu.roll(bt, d, axis=0)
        keep = row >= d
        bt = jnp.where(keep, a * b_s + bt, bt)
        a = jnp.where(keep, a * a_s, a)
        d *= 2
    h = a * h_scr[...] + bt
    rg_ref[...] = (h * jax.nn.gelu(gr_ref[...])).astype(rg_ref.dtype)
    h_scr[...] = h[ts - 1:ts]
    prev_scr[...] = x[ts - SUBLANES:ts]
    ctail_ref[...] = x[ts - SUBLANES:ts]
    htail_ref[...] = h[ts - SUBLANES:ts]


def _rglru_seq(xr, gr, cw, cb, wa_bd, ba, wx_bd, bx, lam, batch, seq):
    ts = min(256, seq)
    ns = seq // ts
    kern = functools.partial(_rglru_seq_kernel, ts=ts)
    cst = lambda a: pl.BlockSpec(a.shape, lambda b, s: (0,) * a.ndim)
    tile = pl.BlockSpec((ts, RG_W), lambda b, s: (b * ns + s, 0))
    tail = pl.BlockSpec((None, SUBLANES, RG_W), lambda b, s: (b, 0, 0))
    return pl.pallas_call(
        kern,
        grid=(batch, ns),
        in_specs=[tile, tile, cst(cw), cst(cb), cst(wa_bd), cst(ba), cst(wx_bd), cst(bx), cst(lam)],
        out_specs=(tile, tail, tail),
        out_shape=(jax.ShapeDtypeStruct((batch * seq, RG_W), BF16),
                   jax.ShapeDtypeStruct((batch, SUBLANES, RG_W), F32),
                   jax.ShapeDtypeStruct((batch, SUBLANES, RG_W), F32)),
        scratch_shapes=[pltpu.VMEM((SUBLANES, RG_W), F32), pltpu.VMEM((1, RG_W), F32)],
        compiler_params=_cparams(("parallel", "arbitrary")),
        name="rglru_seq",
    )(xr, gr, cw, cb, wa_bd, ba, wx_bd, bx, lam)


def _rglru_step_kernel(xr_ref, gr_ref, c0_ref, c1_ref, c2_ref, h0_ref, cw_ref, cb_ref, wa_ref, ba_ref,
                       wx_ref, bx_ref, lam_ref, rg_ref, h_ref):
    x = xr_ref[...]
    cw = cw_ref[...]
    xc = cb_ref[...] + c0_ref[...] * cw[0:1]
    xc = xc + c1_ref[...] * cw[1:2]
    xc = xc + c2_ref[...] * cw[2:3]
    xc = xc + x * cw[3:4]
    a, bt = _rg_gates(xc, wa_ref, ba_ref, wx_ref, bx_ref, lam_ref)
    h = a * h0_ref[...] + bt
    h_ref[...] = h
    rg_ref[...] = (h * jax.nn.gelu(gr_ref[...])).astype(rg_ref.dtype)


def _rglru_step(xr, gr, conv_state, h0, cw, cb, wa_bd, ba, wx_bd, bx, lam):
    nb = xr.shape[0]
    args = (xr, gr, conv_state[:, 0], conv_state[:, 1], conv_state[:, 2], h0,
            cw, cb, wa_bd, ba, wx_bd, bx, lam)
    return pl.pallas_call(
        _rglru_step_kernel,
        grid=(1,),
        in_specs=[_full(a.shape) for a in args],
        out_specs=(_full((nb, RG_W)), _full((nb, RG_W))),
        out_shape=(jax.ShapeDtypeStruct((nb, RG_W), BF16), jax.ShapeDtypeStruct((nb, RG_W), F32)),
        compiler_params=_cparams(("arbitrary",)),
        name="rglru_step",
    )(*args)


def _outproj_router_kernel(att_ref, rg_ref, x_ref, wo_ref, g2_ref, wr_ref, rb_ref, tri_ref,
                           x1_ref, hm8_ref, hmb_ref, idx_ref, gate_ref, rank_ref, cnt_ref,
                           carry_scr, *, tm, tiles_per_super):
    i = pl.program_id(0)

    @pl.when(i % tiles_per_super == 0)
    def _():
        carry_scr[...] = jnp.zeros(carry_scr.shape, F32)

    mixed = (jnp.dot(att_ref[...], wo_ref[:ATT_W, :], preferred_element_type=F32)
             + jnp.dot(rg_ref[...], wo_ref[ATT_W:, :], preferred_element_type=F32))
    x1 = x_ref[...] + mixed
    x1_ref[...] = x1
    ms = jnp.mean(x1 * x1, axis=-1, keepdims=True)
    hm = x1 * lax.rsqrt(ms + EPS) * g2_ref[...]
    hmb = hm.astype(BF16)
    hmb_ref[...] = hmb
    for s in range(SUBLANES):
        hm8_ref[pl.ds(s, tm, stride=SUBLANES), :] = hm[:, s * LANES:(s + 1) * LANES]

    scores = _sigmoid(_nt_dot(wr_ref[...], hmb))
    choice = scores + rb_ref[...]
    ch3 = choice.reshape(N_GROUPS, GROUP_SIZE, tm)
    i8 = lax.broadcasted_iota(jnp.int32, (N_GROUPS, GROUP_SIZE, tm), 1)
    m1 = jnp.max(ch3, axis=1, keepdims=True)
    f1 = jnp.min(jnp.where(ch3 == m1, i8, GROUP_SIZE), axis=1, keepdims=True)
    m2 = jnp.max(jnp.where(i8 == f1, -jnp.inf, ch3), axis=1, keepdims=True)
    gs3 = m1 + m2
    gs = gs3.reshape(N_GROUPS, tm)
    gi = lax.broadcasted_iota(jnp.int32, (N_GROUPS, tm), 0)
    grank = jnp.zeros((N_GROUPS, tm), jnp.int32)
    for g2 in range(N_GROUPS):
        rowv = gs[g2:g2 + 1, :]
        beats = (rowv > gs) | ((rowv == gs) & (g2 < gi))
        grank = grank + beats.astype(jnp.int32)
    gsel = (grank < TOPK_GROUPS).reshape(N_GROUPS, 1, tm)
    masked = jnp.where(gsel, ch3, NEG_INF).reshape(N_EXPERTS, tm)

    ei = lax.broadcasted_iota(jnp.int32, (N_EXPERTS, tm), 0)
    picks, gvals = [], []
    sel = jnp.zeros((N_EXPERTS, tm), F32)
    for _ in range(TOP_K):
        mx = jnp.max(masked, axis=0, keepdims=True)
        first = jnp.min(jnp.where(masked == mx, ei, N_EXPERTS), axis=0, keepdims=True)
        hit = ei == first
        picks.append(first)
        gvals.append(jnp.sum(jnp.where(hit, scores, 0.0), axis=0, keepdims=True))
        sel = jnp.where(hit, 1.0, sel)
        masked = jnp.where(hit, -jnp.inf, masked)
    gsum = gvals[0]
    for kk in range(1, TOP_K):
        gsum = gsum + gvals[kk]

    pos = jnp.dot(sel.astype(BF16), tri_ref[...], preferred_element_type=F32) + carry_scr[...]
    for kk in range(TOP_K):
        idx_ref[kk:kk + 1, :] = picks[kk]
        gate_ref[kk:kk + 1, :] = gvals[kk] / gsum * ROUTE_SCALE
        rk = jnp.sum(jnp.where(ei == picks[kk], pos, 0.0), axis=0, keepdims=True)
        rank_ref[kk:kk + 1, :] = rk.astype(jnp.int32)
    carry = carry_scr[...] + jnp.sum(sel, axis=1, keepdims=True)
    carry_scr[...] = carry
    cnt_ref[...] = jnp.broadcast_to(carry, cnt_ref.shape)


def _outproj_router(att, rg, x2, wo_bf, g2, wr_t, rb_col, tt):
    T, D = x2.shape
    tm = min(256, T)
    tiles_per_super = tt // tm
    n_super = T // tt
    tri = jnp.triu(jnp.ones((tm, tm), BF16), k=1)
    kern = functools.partial(_outproj_router_kernel, tm=tm, tiles_per_super=tiles_per_super)
    row = lambda w: pl.BlockSpec((tm, w), lambda i: (i, 0))
    col = pl.BlockSpec((TOP_K, tm), lambda i: (0, i))
    return pl.pallas_call(
        kern,
        grid=(T // tm,),
        in_specs=[row(ATT_W), row(RG_W), row(D), _full(wo_bf.shape), _full(g2.shape),
                  _full(wr_t.shape), _full(rb_col.shape), _full(tri.shape)],
        out_specs=(row(D), pl.BlockSpec((tm * SUBLANES, LANES), lambda i: (i, 0)), row(D),
                   col, col, col,
                   pl.BlockSpec((None, N_EXPERTS, LANES), lambda i: (i // tiles_per_super, 0, 0))),
        out_shape=(jax.ShapeDtypeStruct((T, D), F32),
                   jax.ShapeDtypeStruct((T * SUBLANES, LANES), F32),
                   jax.ShapeDtypeStruct((T, D), BF16),
                   jax.ShapeDtypeStruct((TOP_K, T), jnp.int32),
                   jax.ShapeDtypeStruct((TOP_K, T), F32),
                   jax.ShapeDtypeStruct((TOP_K, T), jnp.int32),
                   jax.ShapeDtypeStruct((n_super, N_EXPERTS, LANES), F32)),
        scratch_shapes=[pltpu.VMEM((N_EXPERTS, 1), F32)],
        compiler_params=_cparams(("arbitrary",)),
        name="outproj_router",
    )(att, rg, x2, wo_bf, g2, wr_t, rb_col, tri)


def _moe_kernel(cnt_ref, off_ref, hm8_ref, w13_ref, w2_ref, idx_ref, rank_ref, gate_ref, y8_ref,
                dest_v, dest_s, gate_s, row_s, gsort_s, xg, og, sem, *, tt, mch, d_model, d_exp):
    ti = pl.program_id(0)
    e = pl.program_id(1)
    chunks = d_model // LANES

    def rows_at(off):
        return pl.ds(pl.multiple_of(off, chunks), chunks)

    @pl.when(e == 0)
    def _():
        idx = idx_ref[...]
        dest = rank_ref[...]
        for ee in range(N_EXPERTS):
            dest = dest + jnp.where(idx == ee, off_ref[ti * N_EXPERTS + ee], 0)
        dest_v[...] = dest
        copies = []
        for kk in range(TOP_K):
            span = pl.ds(kk * tt, tt)
            copies.append(pltpu.make_async_copy(dest_v.at[kk], dest_s.at[span], sem.at[0]))
            copies.append(pltpu.make_async_copy(gate_ref.at[kk], gate_s.at[span], sem.at[1]))
        for cp in copies:
            cp.start()
        y8_ref[...] = jnp.zeros(y8_ref.shape, F32)
        xg[...] = jnp.zeros(xg.shape, F32)
        og[...] = jnp.zeros(og.shape, F32)
        for cp in copies:
            cp.wait()

        def invert(io, _):
            for u in range(SUBLANES):
                t = io * SUBLANES + u
                picks = [(dest_s[kk * tt + t], gate_s[kk * tt + t]) for kk in range(TOP_K)]
                for slot, gt in picks:
                    row_s[slot] = t * chunks
                    gsort_s[slot] = gt
            return 0

        lax.fori_loop(0, tt // SUBLANES, invert, 0)

    c = cnt_ref[ti * N_EXPERTS + e]
    off = off_ref[ti * N_EXPERTS + e]
    nch = (c + mch - 1) // mch
    g_unroll = 8
    s_unroll = 4

    def chunk(ch, carry):
        base = off + ch * mch
        nvalid = jnp.minimum(c - ch * mch, mch)

        def gather_row(ii):
            xg[rows_at(ii * chunks), :] = hm8_ref[rows_at(row_s[base + ii]), :]

        def gather_group(io, _):
            for u in range(g_unroll):
                gather_row(io * g_unroll + u)
            return 0

        def gather_one(ii, _):
            gather_row(ii)
            return 0

        n_full = nvalid // g_unroll
        lax.fori_loop(0, n_full, gather_group, 0)
        lax.fori_loop(n_full * g_unroll, nvalid, gather_one, 0)
        xb = jnp.concatenate([xg[pl.ds(s, mch, stride=chunks), :] for s in range(chunks)],
                             axis=-1).astype(BF16)
        h13 = jnp.dot(xb, w13_ref[...], preferred_element_type=F32)
        h1 = h13[:, :d_exp]
        hdn = (h1 * _sigmoid(h1)) * h13[:, d_exp:]
        o = jnp.dot(hdn.astype(BF16), w2_ref[...], preferred_element_type=F32)
        for s in range(chunks):
            og[pl.ds(s, mch, stride=chunks), :] = o[:, s * LANES:(s + 1) * LANES]

        def scatter_rows(first, n):
            vals = []
            for u in range(n):
                ii = first + u
                r = row_s[base + ii]
                gt = gsort_s[base + ii]
                vals.append((r, y8_ref[rows_at(r), :] + gt * og[rows_at(ii * chunks), :]))
            for r, v in vals:
                y8_ref[rows_at(r), :] = v

        def scatter_group(io, _):
            scatter_rows(io * s_unroll, s_unroll)
            return 0

        def scatter_one(ii, _):
            scatter_rows(ii, 1)
            return 0

        n_full_s = nvalid // s_unroll
        lax.fori_loop(0, n_full_s, scatter_group, 0)
        lax.fori_loop(n_full_s * s_unroll, nvalid, scatter_one, 0)
        return carry

    lax.fori_loop(0, nch, chunk, 0)


def _moe(hm8, w13, w2, idx, rank, gate, cnt_flat, off_flat, tt, mch):
    T8, _ = hm8.shape
    T = T8 // SUBLANES
    n_tiles = T // tt
    d_model = w13.shape[1]
    d_exp = w2.shape[1]
    assert d_model == SUBLANES * LANES and tt & (tt - 1) == 0 and mch % SUBLANES == 0
    kern = functools.partial(_moe_kernel, tt=tt, mch=mch, d_model=d_model, d_exp=d_exp)
    single = dict(pipeline_mode=pl.Buffered(1))
    picks = pl.BlockSpec((TOP_K, tt), lambda t, e, c, o: (0, t))
    grid_spec = pltpu.PrefetchScalarGridSpec(
        num_scalar_prefetch=2,
        grid=(n_tiles, N_EXPERTS),
        in_specs=[pl.BlockSpec((tt * SUBLANES, LANES), lambda t, e, c, o: (t, 0), **single),
                  pl.BlockSpec((None, d_model, 2 * d_exp), lambda t, e, c, o: (e, 0, 0)),
                  pl.BlockSpec((None, d_exp, d_model), lambda t, e, c, o: (e, 0, 0)),
                  picks, picks, picks],
        out_specs=pl.BlockSpec((tt * SUBLANES, LANES), lambda t, e, c, o: (t, 0), **single),
        scratch_shapes=[pltpu.VMEM((TOP_K, tt), jnp.int32),
                        pltpu.SMEM((tt * TOP_K,), jnp.int32), pltpu.SMEM((tt * TOP_K,), F32),
                        pltpu.SMEM((tt * TOP_K,), jnp.int32), pltpu.SMEM((tt * TOP_K,), F32),
                        pltpu.VMEM((mch * SUBLANES, LANES), F32),
                        pltpu.VMEM((mch * SUBLANES, LANES), F32),
                        pltpu.SemaphoreType.DMA((2,))],
    )
    return pl.pallas_call(
        kern,
        grid_spec=grid_spec,
        out_shape=jax.ShapeDtypeStruct((T8, LANES), F32),
        compiler_params=_cparams(("arbitrary", "arbitrary")),
        name="moe_experts",
    )(cnt_flat, off_flat, hm8, w13, w2, idx, rank, gate)


def _final_kernel(y8_ref, x1_ref, hmb_ref, ws13_ref, ws2_ref, gp_ref, wg_ref, p_ref, wp_ref, o_ref,
                  *, tm, d_sh):
    chunks = x1_ref.shape[1] // LANES
    routed = jnp.concatenate([y8_ref[pl.ds(s, tm, stride=chunks), :] for s in range(chunks)], axis=-1)
    hmb = hmb_ref[...]
    h13 = jnp.dot(hmb, ws13_ref[...], preferred_element_type=F32)
    h1 = h13[:, :d_sh]
    hdn = (h1 * _sigmoid(h1)) * h13[:, d_sh:]
    shared = jnp.dot(hdn.astype(BF16), ws2_ref[...], preferred_element_type=F32)
    x2 = x1_ref[...] + (routed + shared)
    ms = jnp.mean(x2 * x2, axis=-1, keepdims=True)
    hp = (x2 * lax.rsqrt(ms + EPS) * gp_ref[...]).astype(BF16)
    gate = _sigmoid(jnp.dot(hp, wg_ref[...], preferred_element_type=F32))
    proj = jnp.dot(p_ref[...].astype(BF16), wp_ref[...], preferred_element_type=F32)
    o_ref[...] = x2 + gate * proj


def _final(y8, x1, hmb, ws13, ws2, gp, wg, p2, wp):
    T, D = x1.shape
    tm = min(256, T)
    d_sh = ws2.shape[0]
    kern = functools.partial(_final_kernel, tm=tm, d_sh=d_sh)
    row = lambda w: pl.BlockSpec((tm, w), lambda i: (i, 0))
    return pl.pallas_call(
        kern,
        grid=(T // tm,),
        in_specs=[pl.BlockSpec((tm * SUBLANES, LANES), lambda i: (i, 0)), row(D), row(D),
                  _full(ws13.shape), _full(ws2.shape), _full(gp.shape), _full(wg.shape),
                  row(p2.shape[1]), _full(wp.shape)],
        out_specs=row(D),
        out_shape=jax.ShapeDtypeStruct((T, D), F32),
        compiler_params=_cparams(("parallel",)),
        name="final",
    )(y8, x1, hmb, ws13, ws2, gp, wg, p2, wp)


def _block_diag(w):
    nh, d, _ = w.shape
    eye = jnp.eye(nh, dtype=w.dtype)
    return (eye[:, None, :, None] * w[:, :, None, :]).reshape(nh * d, nh * d)


def _prep_weights(i, g_norm1, w_in, g_q, g_k, lam_q1, lam_k1, lam_q2, lam_k2, g_subln, conv_w, conv_b,
                  w_a, b_a, w_x, b_x, lam_rg, w_out, g_norm2, w_router, router_bias, w_e1, w_e3, w_e2,
                  w_s1, w_s3, w_s2, g_ple, w_ple_gate, w_ple_proj):
    reps = ATT_W // HEAD_DIM
    seg = jnp.arange(ATT_W) // HEAD_DIM
    bd = ((seg[:, None] == seg[None, :]).astype(F32) / HEAD_DIM).astype(BF16)
    return dict(
        g1=g_norm1[i][None], w_in=w_in[i].astype(BF16),
        wqt=w_in[i][:, :ATT_W].T.astype(BF16), wvt=w_in[i][:, 2 * ATT_W:3 * ATT_W].T.astype(BF16),
        gq_col=g_q[i][:, None], gk=jnp.tile(g_k[i], reps)[None], bd=bd,
        lq1=lam_q1[i][None], lk1=lam_k1[i][None], lq2=lam_q2[i][None], lk2=lam_k2[i][None],
        gs=g_subln[i][None], gs_col=g_subln[i][:, None],
        cw=conv_w[i], cb=conv_b[i][None],
        wa=_block_diag(w_a[i]).astype(BF16), ba=b_a[i][None],
        wx=_block_diag(w_x[i]).astype(BF16), bx=b_x[i][None], lam_rg=lam_rg[i][None],
        wo=w_out[i].astype(BF16), g2=g_norm2[i][None],
        wr_t=w_router[i].T.astype(BF16), rb=router_bias[i][:, None],
        w13=jnp.concatenate([w_e1[i], w_e3[i]], axis=-1).astype(BF16), w2=w_e2[i].astype(BF16),
        ws13=jnp.concatenate([w_s1[i], w_s3[i]], axis=-1).astype(BF16), ws2=w_s2[i].astype(BF16),
        gp=g_ple[i][None], wg=w_ple_gate[i].astype(BF16), wp=w_ple_proj[i].astype(BF16),
    )


def _moe_stage(att, rg, x2, p2, W, tt, mch):
    T = x2.shape[0]
    x1, hm8, hmb, idx, gate, rank, cnt = _outproj_router(att, rg, x2, W['wo'], W['g2'], W['wr_t'],
                                                         W['rb'], tt)
    cnt_i = cnt[:, :, 0].astype(jnp.int32)
    off_i = jnp.cumsum(cnt_i, axis=1) - cnt_i
    y8 = _moe(hm8, W['w13'], W['w2'], idx, rank, gate, cnt_i.reshape(-1), off_i.reshape(-1), tt, mch)
    return _final(y8, x1, hmb, W['ws13'], W['ws2'], W['gp'], W['wg'], p2, W['wp'])


def _slopes_col():
    h = jnp.arange(N_MAPS) // 2 + 1
    return (LOG2E * jnp.exp2(-8.0 * h.astype(F32) / N_HEADS))[:, None]


def kernel(x_prompt, x_sample, cache_k, cache_v, state_conv, state_h, page_table, p_prompt, p_sample, g_norm1, w_in, g_q, g_k, lam_q1, lam_k1, lam_q2, lam_k2, g_subln, conv_w, conv_b, w_a, b_a, w_x, b_x, lam_rg, w_out, g_norm2, w_router, router_bias, w_e1, w_e3, w_e2, w_s1, w_s3, w_s2, g_ple, w_ple_gate, w_ple_proj):
    depth = w_in.shape[0]
    B, S, D = x_prompt.shape
    NB, NS, _ = x_sample.shape
    assert NS == 1
    n_pool = cache_k.shape[1]
    yp = x_prompt.reshape(B * S, D)
    ys = x_sample.reshape(NB * NS, D)
    kp_l, vp_l, cp_l, hp_l, ks_l, vs_l, cs_l, hs_l = [], [], [], [], [], [], [], []
    for i in range(depth):
        W = _prep_weights(i, g_norm1, w_in, g_q, g_k, lam_q1, lam_k1, lam_q2, lam_k2, g_subln, conv_w,
                          conv_b, w_a, b_a, w_x, b_x, lam_rg, w_out, g_norm2, w_router, router_bias,
                          w_e1, w_e3, w_e2, w_s1, w_s3, w_s2, g_ple, w_ple_gate, w_ple_proj)
        lam_init = 0.8 - 0.6 * math.exp(-0.3 * i)
        lam_vecs = (W['lq1'], W['lk1'], W['lq2'], W['lk2'])
        inproj_w = (W['g1'], W['w_in'], W['wqt'], W['wvt'], W['gq_col'], W['gk'], W['bd'])

        qt, k, kb, v, vte, xr, gr = _inproj(yp, *inproj_w)
        att = _prompt_attention(qt, kb, vte, *lam_vecs, W['gs_col'], B, S, lam_init)
        rg, ctail, htail = _rglru_seq(xr, gr, W['cw'], W['cb'], W['wa'], W['ba'], W['wx'], W['bx'],
                                      W['lam_rg'], B, S)
        tt_p = min(4096, B * S)
        mch_p = tt_p // SUBLANES + tt_p // 64
        yp = _moe_stage(att, rg, yp, p_prompt[i].reshape(B * S, -1), W, tt_p, mch_p)
        kp_l.append(k.reshape(B, S, N_HEADS, 2, HEAD_DIM))
        vp_l.append(v.reshape(B, S, N_HEADS, V_DIM))
        cp_l.append(ctail[:, SUBLANES - (CONV_W - 1):])
        hp_l.append(htail[:, SUBLANES - 1])

        qt, k, kb, v, vte, xr, gr = _inproj(ys, *inproj_w)
        qh = qt.T.reshape(NB, N_MAPS, V_DIM)
        hsel = (jnp.arange(N_MAPS)[:, None] // 2 == jnp.arange(N_HEADS)[None, :])
        qblk = (qh[:, :, None, :] * hsel[None, :, :, None].astype(BF16)).reshape(NB, N_MAPS, ATT_W)
        v_rows = jnp.repeat(v.reshape(NB, N_HEADS, V_DIM), 2, axis=1)
        cache_kt = jnp.transpose(cache_k[i], (0, 2, 3, 4, 1)).reshape(n_pool, ATT_W, PAGE)
        cache_v2 = cache_v[i].reshape(n_pool, PAGE * N_HEADS, V_DIM)
        att_s = _decode_attention(qblk, k.reshape(NB, 1, ATT_W), v_rows, _slopes_col(), *lam_vecs,
                                  W['gs'], cache_kt, cache_v2, page_table, lam_init)
        rg_s, h_s = _rglru_step(xr, gr, state_conv[i], state_h[i], W['cw'], W['cb'], W['wa'], W['ba'],
                                W['wx'], W['bx'], W['lam_rg'])
        ys = _moe_stage(att_s.reshape(NB, ATT_W).astype(BF16), rg_s, ys,
                        p_sample[i].reshape(NB * NS, -1), W, NB, 32)
        ks_l.append(k.reshape(NB, NS, N_HEADS, 2, HEAD_DIM))
        vs_l.append(v.reshape(NB, NS, N_HEADS, V_DIM))
        cs_l.append(jnp.concatenate([state_conv[i][:, 1:], xr[:, None, :]], axis=1))
        hs_l.append(h_s)
    return (yp.reshape(B, S, D), ys.reshape(NB, NS, D), jnp.stack(kp_l), jnp.stack(vp_l),
            jnp.stack(cp_l), jnp.stack(hp_l), jnp.stack(ks_l), jnp.stack(vs_l), jnp.stack(cs_l),
            jnp.stack(hs_l))
```

```python
import functools
import math

import jax
import jax.numpy as jnp
from jax import lax
from jax.experimental import pallas as pl
from jax.experimental.pallas import tpu as pltpu

F32 = jnp.float32
BF16 = jnp.bfloat16

EPS = 1e-6
NEG_INF = -1e30
LANES = 128
SUBLANES = 8
N_HEADS = 4
HEAD_DIM = 64
V_DIM = 2 * HEAD_DIM
N_MAPS = 2 * N_HEADS
ATT_W = N_HEADS * V_DIM
RG_W = 512
RG_C = 8.0
CONV_W = 4
N_EXPERTS = 64
N_GROUPS = 8
GROUP_SIZE = N_EXPERTS // N_GROUPS
TOPK_GROUPS = 4
TOP_K = 8
ROUTE_SCALE = 2.5
ATT_SCALE = HEAD_DIM ** -0.5
LOG2E = math.log2(math.e)
VE_ROWS = V_DIM + 16
PAGE = 128
VMEM_LIMIT = 56 * 1024 * 1024


def _cparams(sem):
    return pltpu.CompilerParams(dimension_semantics=sem, vmem_limit_bytes=VMEM_LIMIT)


def _sigmoid(x):
    return 1.0 / (1.0 + jnp.exp(-x))


def _nt_dot(a, b):
    return lax.dot_general(a, b, (((1,), (1,)), ((), ())), preferred_element_type=F32)


def _full(shape):
    n = len(shape)
    return pl.BlockSpec(shape, lambda *_: (0,) * n)


def _alibi_slopes2():
    return [LOG2E * 2.0 ** (-8.0 * (hh + 1) / N_HEADS) for hh in range(N_HEADS)]


def _inproj_kernel(x_ref, g1_ref, w_ref, wqt_ref, wkt_ref, wvt_ref, gq_ref, gk_ref,
                   qt_ref, kt_ref, kbe_ref, v4_ref, vte_ref, xr_ref, gr_ref, *, tb):
    x = x_ref[...]
    tm = x.shape[0]
    ms = jnp.mean(x * x, axis=-1, keepdims=True)
    h = (x * lax.rsqrt(ms + EPS) * g1_ref[...]).astype(BF16)

    def proj(lo, hi):
        return jnp.dot(h, w_ref[:, lo:hi], preferred_element_type=F32)

    def feature_major_normed(wt_ref, g_ref):
        z = _nt_dot(wt_ref[...], h).reshape(N_MAPS, HEAD_DIM, tm)
        zms = jnp.mean(z * z, axis=1, keepdims=True)
        return z * lax.rsqrt(zms + EPS) * g_ref[...][None]

    qn = feature_major_normed(wqt_ref, gq_ref) * (ATT_SCALE * LOG2E)
    kn = feature_major_normed(wkt_ref, gk_ref)
    kt_ref[...] = kn.reshape(N_MAPS * HEAD_DIM, tm)

    row8 = lax.broadcasted_iota(jnp.int32, (SUBLANES, tm), 0)
    pad = jnp.zeros((V_DIM - HEAD_DIM - SUBLANES, tm), F32)
    q_tail = jnp.concatenate([jnp.where(row8 < 3, 1.0, 0.0), pad], axis=0).astype(BF16)
    tok = pl.program_id(0) * tm + lax.broadcasted_iota(jnp.int32, (1, tm), 1)
    k_local = (tok % tb).astype(F32)
    slopes2 = _alibi_slopes2()
    kte = []
    for mm in range(N_MAPS):
        qt_ref[mm * V_DIM:mm * V_DIM + HEAD_DIM, :] = qn[mm].astype(BF16)
        qt_ref[mm * V_DIM + HEAD_DIM:(mm + 1) * V_DIM, :] = q_tail
        val = slopes2[mm // 2] * k_local
        hi = val.astype(BF16).astype(F32)
        mid = (val - hi).astype(BF16).astype(F32)
        lo = ((val - hi) - mid).astype(BF16).astype(F32)
        k_tail = jnp.where(row8 == 0, hi, jnp.where(row8 == 1, mid, jnp.where(row8 == 2, lo, 0.0)))
        kte += [kn[mm], k_tail, pad]
    kbe_ref[...] = jnp.concatenate(kte, axis=0).T.astype(BF16)

    v = proj(2 * ATT_W, 3 * ATT_W)
    for hh in range(N_HEADS):
        v4_ref[pl.ds(hh, tm, stride=N_HEADS), :] = v[:, hh * V_DIM:(hh + 1) * V_DIM]
    vt = _nt_dot(wvt_ref[...], h).astype(BF16)
    ones = jnp.ones((VE_ROWS - V_DIM, tm), BF16)
    for hh in range(N_HEADS):
        vte_ref[hh * VE_ROWS:hh * VE_ROWS + V_DIM, :] = vt[hh * V_DIM:(hh + 1) * V_DIM]
        vte_ref[hh * VE_ROWS + V_DIM:(hh + 1) * VE_ROWS, :] = ones
    xr_ref[...] = proj(3 * ATT_W, 3 * ATT_W + RG_W)
    gr_ref[...] = proj(3 * ATT_W + RG_W, 3 * ATT_W + 2 * RG_W)


def _inproj(x2, batch, tb, g1, w_in_bf, wqt, wkt, wvt, gq_col, gk_col):
    T, D = x2.shape
    seq = T // batch
    tm = min(256, seq)
    tiles = seq // tm
    row = lambda w: pl.BlockSpec((tm, w), lambda i: (i, 0))
    col = lambda r: pl.BlockSpec((r, tm), lambda i: (0, i))
    out_shape = (
        jax.ShapeDtypeStruct((N_MAPS * V_DIM, T), BF16),
        jax.ShapeDtypeStruct((batch, ATT_W, seq), F32),
        jax.ShapeDtypeStruct((T, N_MAPS * V_DIM), BF16),
        jax.ShapeDtypeStruct((T * N_HEADS, V_DIM), F32),
        jax.ShapeDtypeStruct((N_HEADS * VE_ROWS, T), BF16),
        jax.ShapeDtypeStruct((T, RG_W), F32), jax.ShapeDtypeStruct((T, RG_W), F32),
    )
    args = (x2, g1, w_in_bf, wqt, wkt, wvt, gq_col, gk_col)
    return pl.pallas_call(
        functools.partial(_inproj_kernel, tb=tb),
        grid=(T // tm,),
        in_specs=[row(D)] + [_full(a.shape) for a in args[1:]],
        out_specs=(col(N_MAPS * V_DIM),
                   pl.BlockSpec((None, ATT_W, tm), lambda i: (i // tiles, 0, i % tiles)),
                   row(N_MAPS * V_DIM),
                   pl.BlockSpec((tm * N_HEADS, V_DIM), lambda i: (i, 0)),
                   col(N_HEADS * VE_ROWS), row(RG_W), row(RG_W)),
        out_shape=out_shape,
        compiler_params=_cparams(("parallel",)),
        name="inproj",
    )(*args)


def _lam_value(lq1, lk1, lq2, lk2, lam_init):
    s1 = jnp.sum(lq1 * lk1, axis=-1, keepdims=True)
    s2 = jnp.sum(lq2 * lk2, axis=-1, keepdims=True)
    return jnp.exp(s1) - jnp.exp(s2) + lam_init


def _subln(o, g, lam_init):
    ms = jnp.mean(o * o, axis=-1, keepdims=True)
    return (o * lax.rsqrt(ms + EPS) * g) * (1.0 - lam_init)


def _prompt_attn_kernel(it_ref, jt_ref, qt_ref, kbe_ref, vte_ref, lq1_ref, lk1_ref, lq2_ref, lk2_ref,
                        gsc_ref, o_ref, m_scr, acc_scr, mask_scr, *, tb, strip, lam_init):
    pidx = pl.program_id(1)
    i = it_ref[pidx]
    j = jt_ref[pidx]
    slopes2 = _alibi_slopes2()

    @pl.when(pidx == 0)
    def _():
        kl = lax.broadcasted_iota(jnp.int32, (tb, tb), 0)
        ql = lax.broadcasted_iota(jnp.int32, (tb, tb), 1)
        mask_scr[...] = jnp.where(kl <= ql, 0.0, NEG_INF)

    @pl.when(j == 0)
    def _():
        m_scr[...] = jnp.full(m_scr.shape, NEG_INF, F32)
        acc_scr[...] = jnp.zeros(acc_scr.shape, F32)

    gap = ((i - j) * tb).astype(F32)

    def update(diagonal):
        for mm in range(N_MAPS):
            hh = mm // 2
            off = -slopes2[hh] * gap
            for s0 in range(0, tb, strip):
                nk = s0 + strip if diagonal else tb
                lanes = slice(s0, s0 + strip)
                u = jnp.dot(kbe_ref[:nk, mm * V_DIM:(mm + 1) * V_DIM],
                            qt_ref[mm * V_DIM:(mm + 1) * V_DIM, lanes], preferred_element_type=F32)
                if diagonal:
                    u = u + mask_scr[:nk, lanes]
                m_prev = m_scr[mm, :, lanes]
                m_new = jnp.maximum(m_prev, jnp.max(u, axis=0, keepdims=True) + off)
                alpha = jnp.exp2(m_prev - m_new)
                p = jnp.exp2(u - (m_new - off)).astype(BF16)
                acc_scr[mm, :, lanes] = alpha * acc_scr[mm, :, lanes] + jnp.dot(
                    vte_ref[hh * VE_ROWS:(hh + 1) * VE_ROWS, :nk], p, preferred_element_type=F32)
                m_scr[mm, :, lanes] = m_new

    @pl.when(j < i)
    def _():
        update(False)

    @pl.when(j == i)
    def _():
        update(True)
        lam = _lam_value(lq1_ref[...], lk1_ref[...], lq2_ref[...], lk2_ref[...], lam_init)
        for hh in range(N_HEADS):
            a1 = acc_scr[2 * hh]
            a2 = acc_scr[2 * hh + 1]
            o1 = a1[:V_DIM] / a1[V_DIM:V_DIM + 1]
            o2 = a2[:V_DIM] / a2[V_DIM:V_DIM + 1]
            o = o1 - lam * o2
            ms = jnp.mean(o * o, axis=0, keepdims=True)
            y = (o * lax.rsqrt(ms + EPS) * gsc_ref[...]) * (1.0 - lam_init)
            o_ref[:, hh * V_DIM:(hh + 1) * V_DIM] = y.T.astype(o_ref.dtype)


def _prompt_attention(qt, kbe, vte, lq1, lk1, lq2, lk2, gs_col, batch, seq, tb, lam_init):
    nb = seq // tb
    strip = tb
    pairs = [(i, j) for i in range(nb) for j in range(i + 1)]
    i_tab = jnp.asarray([p[0] for p in pairs], jnp.int32)
    j_tab = jnp.asarray([p[1] for p in pairs], jnp.int32)
    cst = lambda a: pl.BlockSpec(a.shape, lambda b, p, it, jt: (0,) * a.ndim)
    kern = functools.partial(_prompt_attn_kernel, tb=tb, strip=strip, lam_init=lam_init)
    grid_spec = pltpu.PrefetchScalarGridSpec(
        num_scalar_prefetch=2,
        grid=(batch, len(pairs)),
        in_specs=[pl.BlockSpec((N_MAPS * V_DIM, tb), lambda b, p, it, jt: (0, b * nb + it[p])),
                  pl.BlockSpec((tb, N_MAPS * V_DIM), lambda b, p, it, jt: (b * nb + jt[p], 0)),
                  pl.BlockSpec((N_HEADS * VE_ROWS, tb), lambda b, p, it, jt: (0, b * nb + jt[p])),
                  cst(lq1), cst(lk1), cst(lq2), cst(lk2), cst(gs_col)],
        out_specs=pl.BlockSpec((tb, ATT_W), lambda b, p, it, jt: (b * nb + it[p], 0)),
        scratch_shapes=[pltpu.VMEM((N_MAPS, 1, tb), F32),
                        pltpu.VMEM((N_MAPS, VE_ROWS, tb), F32),
                        pltpu.VMEM((tb, tb), F32)],
    )
    return pl.pallas_call(
        kern,
        grid_spec=grid_spec,
        out_shape=jax.ShapeDtypeStruct((batch * seq, ATT_W), BF16),
        compiler_params=_cparams(("arbitrary", "arbitrary")),
        name="prompt_attn",
    )(i_tab, j_tab, qt, kbe, vte, lq1, lk1, lq2, lk2, gs_col)


def _decode_attn_kernel(pt_ref, qb_ref, kn_ref, vn_ref, sl_ref, lq1_ref, lk1_ref, lq2_ref, lk2_ref,
                        gs_ref, *refs, pps, past, lam_init):
    kp_refs = refs[:pps]
    vp_refs = refs[pps:2 * pps]
    o_ref, m_scr, l_scr, acc_scr = refs[2 * pps:]
    g = pl.program_id(1)
    ng = pl.num_programs(1)

    @pl.when(g == 0)
    def _():
        m_scr[...] = jnp.full(m_scr.shape, NEG_INF, F32)
        l_scr[...] = jnp.zeros(l_scr.shape, F32)
        acc_scr[...] = jnp.zeros(acc_scr.shape, F32)

    q = qb_ref[...]
    slopes = sl_ref[...]
    lane = lax.broadcasted_iota(jnp.int32, (1, PAGE), 1)
    head_of_row = lax.broadcasted_iota(jnp.int32, (N_MAPS, 1), 0) // 2
    m_run = m_scr[...]
    l_run = l_scr[...]
    acc = acc_scr[...]
    scores = []
    for pi in range(pps):
        kpos = (g * pps + pi) * PAGE + lane
        dist = (past - kpos).astype(F32)
        s = jnp.dot(q, kp_refs[pi][...].astype(BF16), preferred_element_type=F32)
        scores.append(s - slopes * dist)
    smax = scores[0]
    for s in scores[1:]:
        smax = jnp.maximum(smax, s)
    m_new = jnp.maximum(m_run, jnp.max(smax, axis=-1, keepdims=True))
    alpha = jnp.exp2(m_run - m_new)
    psum = jnp.zeros((N_MAPS, PAGE), F32)
    pvs = [jnp.zeros((N_MAPS, V_DIM), F32) for _ in range(N_HEADS)]
    for pi in range(pps):
        p = jnp.exp2(scores[pi] - m_new)
        psum = psum + p
        pb = p.astype(BF16)
        for hh in range(N_HEADS):
            vh = vp_refs[pi][pl.ds(hh, PAGE, stride=N_HEADS), :].astype(BF16)
            pvs[hh] = pvs[hh] + jnp.dot(pb, vh, preferred_element_type=F32)
    pv = pvs[0]
    for hh in range(1, N_HEADS):
        pv = jnp.where(head_of_row == hh, pvs[hh], pv)
    m_scr[...] = m_new
    l_run = alpha * l_run + jnp.sum(psum, axis=-1, keepdims=True)
    acc = alpha * acc + pv
    l_scr[...] = l_run
    acc_scr[...] = acc
    m_run = m_new

    @pl.when(g == ng - 1)
    def _():
        kn = kn_ref[...].astype(BF16).astype(F32)
        vn = vn_ref[...].astype(BF16).astype(F32)
        s = jnp.sum(q.astype(F32) * kn, axis=-1, keepdims=True)
        m_new = jnp.maximum(m_run, s)
        alpha = jnp.exp2(m_run - m_new)
        p = jnp.exp2(s - m_new)
        l_fin = alpha * l_run + p
        acc_fin = alpha * acc + p.astype(BF16).astype(F32) * vn
        lam = _lam_value(lq1_ref[...], lk1_ref[...], lq2_ref[...], lk2_ref[...], lam_init)
        outn = acc_fin / l_fin
        for hh in range(N_HEADS):
            o = outn[2 * hh:2 * hh + 1] - lam * outn[2 * hh + 1:2 * hh + 2]
            o_ref[:, hh * V_DIM:(hh + 1) * V_DIM] = _subln(o, gs_ref[...], lam_init)


def _decode_attention(qblk, k_new, v_rows, slopes, lq1, lk1, lq2, lk2, gs, cache_kt, cache_v2,
                      page_table, lam_init):
    nb, n_pages = page_table.shape
    pps = 8
    while n_pages % pps:
        pps //= 2
    past = n_pages * PAGE
    pt_flat = page_table.reshape(-1)

    def page_map(pi):
        return lambda b, g, pt: (pt[b * n_pages + g * pps + pi], 0, 0)

    k_specs = [pl.BlockSpec((None, ATT_W, PAGE), page_map(pi)) for pi in range(pps)]
    v_specs = [pl.BlockSpec((None, PAGE * N_HEADS, V_DIM), page_map(pi)) for pi in range(pps)]
    cst = lambda shape: pl.BlockSpec(shape, lambda b, g, pt: (0,) * len(shape))
    kern = functools.partial(_decode_attn_kernel, pps=pps, past=past, lam_init=lam_init)
    grid_spec = pltpu.PrefetchScalarGridSpec(
        num_scalar_prefetch=1,
        grid=(nb, n_pages // pps),
        in_specs=[pl.BlockSpec((None, N_MAPS, ATT_W), lambda b, g, pt: (b, 0, 0)),
                  pl.BlockSpec((None, 1, ATT_W), lambda b, g, pt: (b, 0, 0)),
                  pl.BlockSpec((None, N_MAPS, V_DIM), lambda b, g, pt: (b, 0, 0)),
                  cst(slopes.shape), cst(lq1.shape), cst(lk1.shape), cst(lq2.shape), cst(lk2.shape),
                  cst(gs.shape)] + k_specs + v_specs,
        out_specs=pl.BlockSpec((None, 1, ATT_W), lambda b, g, pt: (b, 0, 0)),
        scratch_shapes=[pltpu.VMEM((N_MAPS, 1), F32), pltpu.VMEM((N_MAPS, 1), F32),
                        pltpu.VMEM((N_MAPS, V_DIM), F32)],
    )
    return pl.pallas_call(
        kern,
        grid_spec=grid_spec,
        out_shape=jax.ShapeDtypeStruct((nb, 1, ATT_W), F32),
        compiler_params=_cparams(("parallel", "arbitrary")),
        name="decode_attn",
    )(pt_flat, qblk, k_new, v_rows, slopes, lq1, lk1, lq2, lk2, gs,
      *([cache_kt] * pps), *([cache_v2] * pps))


def _expm1(y):
    acc = 1.0 + y * (1.0 / 15.0)
    for n in range(14, 1, -1):
        acc = 1.0 + (y * (1.0 / n)) * acc
    return jnp.where(jnp.abs(y) < 0.25, y * acc, jnp.exp(y) - 1.0)


def _log1p(z):
    return jnp.where(z < 1e-4, z * (1.0 - z * (0.5 - z * (1.0 / 3.0))), jnp.log(1.0 + z))


def _rg_gates(xc, wa_ref, ba_ref, wx_ref, bx_ref, lam_ref):
    xcb = xc.astype(BF16)
    r = _sigmoid(jnp.dot(xcb, wa_ref[...], preferred_element_type=F32) + ba_ref[...])
    ig = _sigmoid(jnp.dot(xcb, wx_ref[...], preferred_element_type=F32) + bx_ref[...])
    nl = -lam_ref[...]
    sp = jnp.maximum(nl, 0.0) + _log1p(jnp.exp(-jnp.abs(nl)))
    log_a = -RG_C * r * sp
    a = jnp.exp(log_a)
    bt = jnp.sqrt(-_expm1(2.0 * log_a)) * (ig * xc)
    return a, bt


def _rglru_seq_kernel(xr_ref, gr_ref, cw_ref, cb_ref, wa_ref, ba_ref, wx_ref, bx_ref, lam_ref,
                      rg_ref, ctail_ref, htail_ref, prev_scr, h_scr, *, ts):
    s = pl.program_id(1)

    @pl.when(s == 0)
    def _():
        prev_scr[...] = jnp.zeros(prev_scr.shape, F32)
        h_scr[...] = jnp.zeros(h_scr.shape, F32)

    x = xr_ref[...]
    prev = prev_scr[...]
    row8 = lax.broadcasted_iota(jnp.int32, (SUBLANES, 1), 0)
    cw = cw_ref[...]
    xc = cb_ref[...] + x * cw[CONV_W - 1:CONV_W]
    for jj in range(1, CONV_W):
        xs = pltpu.roll(x, jj, axis=0)
        ps = pltpu.roll(prev, jj, axis=0)
        top = jnp.where(row8 < jj, ps, xs[:SUBLANES])
        xs = jnp.concatenate([top, xs[SUBLANES:]], axis=0)
        xc = xc + xs * cw[CONV_W - 1 - jj:CONV_W - jj]
    a, bt = _rg_gates(xc, wa_ref, ba_ref, wx_ref, bx_ref, lam_ref)
    row = lax.broadcasted_iota(jnp.int32, (ts, 1), 0)
    d = 1
    while d < ts:
        a_s = pltpu.roll(a, d, axis=0)
        b_s = pltpu.roll(bt, d, axis=0)
        keep = row >= d
        bt = jnp.where(keep, a * b_s + bt, bt)
        a = jnp.where(keep, a * a_s, a)
        d *= 2
    h = a * h_scr[...] + bt
    rg_ref[...] = (h * jax.nn.gelu(gr_ref[...])).astype(rg_ref.dtype)
    h_scr[...] = h[ts - 1:ts]
    prev_scr[...] = x[ts - SUBLANES:ts]
    ctail_ref[...] = x[ts - SUBLANES:ts]
    htail_ref[...] = h[ts - SUBLANES:ts]


def _rglru_seq(xr, gr, cw, cb, wa_bd, ba, wx_bd, bx, lam, batch, seq):
    ts = min(256, seq)
    ns = seq // ts
    kern = functools.partial(_rglru_seq_kernel, ts=ts)
    cst = lambda a: pl.BlockSpec(a.shape, lambda b, s: (0,) * a.ndim)
    tile = pl.BlockSpec((ts, RG_W), lambda b, s: (b * ns + s, 0))
    tail = pl.BlockSpec((None, SUBLANES, RG_W), lambda b, s: (b, 0, 0))
    return pl.pallas_call(
        kern,
        grid=(batch, ns),
        in_specs=[tile, tile, cst(cw), cst(cb), cst(wa_bd), cst(ba), cst(wx_bd), cst(bx), cst(lam)],
        out_specs=(tile, tail, tail),
        out_shape=(jax.ShapeDtypeStruct((batch * seq, RG_W), BF16),
                   jax.ShapeDtypeStruct((batch, SUBLANES, RG_W), F32),
                   jax.ShapeDtypeStruct((batch, SUBLANES, RG_W), F32)),
        scratch_shapes=[pltpu.VMEM((SUBLANES, RG_W), F32), pltpu.VMEM((1, RG_W), F32)],
        compiler_params=_cparams(("parallel", "arbitrary")),
        name="rglru_seq",
    )(xr, gr, cw, cb, wa_bd, ba, wx_bd, bx, lam)


def _rglru_step_kernel(xr_ref, gr_ref, c0_ref, c1_ref, c2_ref, h0_ref, cw_ref, cb_ref, wa_ref, ba_ref,
                       wx_ref, bx_ref, lam_ref, rg_ref, h_ref):
    x = xr_ref[...]
    cw = cw_ref[...]
    xc = cb_ref[...] + c0_ref[...] * cw[0:1]
    xc = xc + c1_ref[...] * cw[1:2]
    xc = xc + c2_ref[...] * cw[2:3]
    xc = xc + x * cw[3:4]
    a, bt = _rg_gates(xc, wa_ref, ba_ref, wx_ref, bx_ref, lam_ref)
    h = a * h0_ref[...] + bt
    h_ref[...] = h
    rg_ref[...] = (h * jax.nn.gelu(gr_ref[...])).astype(rg_ref.dtype)


def _rglru_step(xr, gr, conv_state, h0, cw, cb, wa_bd, ba, wx_bd, bx, lam):
    nb = xr.shape[0]
    args = (xr, gr, conv_state[:, 0], conv_state[:, 1], conv_state[:, 2], h0,
            cw, cb, wa_bd, ba, wx_bd, bx, lam)
    return pl.pallas_call(
        _rglru_step_kernel,
        grid=(1,),
        in_specs=[_full(a.shape) for a in args],
        out_specs=(_full((nb, RG_W)), _full((nb, RG_W))),
        out_shape=(jax.ShapeDtypeStruct((nb, RG_W), BF16), jax.ShapeDtypeStruct((nb, RG_W), F32)),
        compiler_params=_cparams(("arbitrary",)),
        name="rglru_step",
    )(*args)


def _outproj_router_kernel(att_ref, rg_ref, x_ref, wo_ref, g2_ref, wr_ref, rb_ref, tri_ref,
                           x1_ref, hm8_ref, hmb_ref, idx_ref, gate_ref, rank_ref, cnt_ref,
                           carry_scr, *, tm, tiles_per_super):
    i = pl.program_id(0)

    @pl.when(i % tiles_per_super == 0)
    def _():
        carry_scr[...] = jnp.zeros(carry_scr.shape, F32)

    mixed = (jnp.dot(att_ref[...], wo_ref[:ATT_W, :], preferred_element_type=F32)
             + jnp.dot(rg_ref[...], wo_ref[ATT_W:, :], preferred_element_type=F32))
    x1 = x_ref[...] + mixed
    x1_ref[...] = x1
    ms = jnp.mean(x1 * x1, axis=-1, keepdims=True)
    hm = x1 * lax.rsqrt(ms + EPS) * g2_ref[...]
    hmb = hm.astype(BF16)
    hmb_ref[...] = hmb
    for s in range(SUBLANES):
        hm8_ref[pl.ds(s, tm, stride=SUBLANES), :] = hm[:, s * LANES:(s + 1) * LANES]

    scores = _sigmoid(_nt_dot(wr_ref[...], hmb))
    choice = scores + rb_ref[...]
    ch3 = choice.reshape(N_GROUPS, GROUP_SIZE, tm)
    i8 = lax.broadcasted_iota(jnp.int32, (N_GROUPS, GROUP_SIZE, tm), 1)
    m1 = jnp.max(ch3, axis=1, keepdims=True)
    f1 = jnp.min(jnp.where(ch3 == m1, i8, GROUP_SIZE), axis=1, keepdims=True)
    m2 = jnp.max(jnp.where(i8 == f1, -jnp.inf, ch3), axis=1, keepdims=True)
    gs3 = m1 + m2
    gs = gs3.reshape(N_GROUPS, tm)
    gi = lax.broadcasted_iota(jnp.int32, (N_GROUPS, tm), 0)
    grank = jnp.zeros((N_GROUPS, tm), jnp.int32)
    for g2 in range(N_GROUPS):
        rowv = gs[g2:g2 + 1, :]
        beats = (rowv > gs) | ((rowv == gs) & (g2 < gi))
        grank = grank + beats.astype(jnp.int32)
    gsel = (grank < TOPK_GROUPS).reshape(N_GROUPS, 1, tm)
    masked = jnp.where(gsel, ch3, NEG_INF).reshape(N_EXPERTS, tm)

    ei = lax.broadcasted_iota(jnp.int32, (N_EXPERTS, tm), 0)
    picks, gvals = [], []
    sel = jnp.zeros((N_EXPERTS, tm), F32)
    for _ in range(TOP_K):
        mx = jnp.max(masked, axis=0, keepdims=True)
        first = jnp.min(jnp.where(masked == mx, ei, N_EXPERTS), axis=0, keepdims=True)
        hit = ei == first
        picks.append(first)
        gvals.append(jnp.sum(jnp.where(hit, scores, 0.0), axis=0, keepdims=True))
        sel = jnp.where(hit, 1.0, sel)
        masked = jnp.where(hit, -jnp.inf, masked)
    gsum = gvals[0]
    for kk in range(1, TOP_K):
        gsum = gsum + gvals[kk]

    pos = jnp.dot(sel.astype(BF16), tri_ref[...], preferred_element_type=F32) + carry_scr[...]
    for kk in range(TOP_K):
        idx_ref[kk:kk + 1, :] = picks[kk]
        gate_ref[kk:kk + 1, :] = gvals[kk] / gsum * ROUTE_SCALE
        rk = jnp.sum(jnp.where(ei == picks[kk], pos, 0.0), axis=0, keepdims=True)
        rank_ref[kk:kk + 1, :] = rk.astype(jnp.int32)
    carry = carry_scr[...] + jnp.sum(sel, axis=1, keepdims=True)
    carry_scr[...] = carry
    cnt_ref[...] = jnp.broadcast_to(carry, cnt_ref.shape)


def _outproj_router(att, rg, x2, wo_bf, g2, wr_t, rb_col, tt):
    T, D = x2.shape
    tm = min(256, T)
    tiles_per_super = tt // tm
    n_super = T // tt
    tri = jnp.triu(jnp.ones((tm, tm), BF16), k=1)
    kern = functools.partial(_outproj_router_kernel, tm=tm, tiles_per_super=tiles_per_super)
    row = lambda w: pl.BlockSpec((tm, w), lambda i: (i, 0))
    col = pl.BlockSpec((TOP_K, tm), lambda i: (0, i))
    return pl.pallas_call(
        kern,
        grid=(T // tm,),
        in_specs=[row(ATT_W), row(RG_W), row(D), _full(wo_bf.shape), _full(g2.shape),
                  _full(wr_t.shape), _full(rb_col.shape), _full(tri.shape)],
        out_specs=(row(D), pl.BlockSpec((tm * SUBLANES, LANES), lambda i: (i, 0)), row(D),
                   col, col, col,
                   pl.BlockSpec((None, N_EXPERTS, LANES), lambda i: (i // tiles_per_super, 0, 0))),
        out_shape=(jax.ShapeDtypeStruct((T, D), F32),
                   jax.ShapeDtypeStruct((T * SUBLANES, LANES), F32),
                   jax.ShapeDtypeStruct((T, D), BF16),
                   jax.ShapeDtypeStruct((TOP_K, T), jnp.int32),
                   jax.ShapeDtypeStruct((TOP_K, T), F32),
                   jax.ShapeDtypeStruct((TOP_K, T), jnp.int32),
                   jax.ShapeDtypeStruct((n_super, N_EXPERTS, LANES), F32)),
        scratch_shapes=[pltpu.VMEM((N_EXPERTS, 1), F32)],
        compiler_params=_cparams(("arbitrary",)),
        name="outproj_router",
    )(att, rg, x2, wo_bf, g2, wr_t, rb_col, tri)


def _moe_kernel(cnt_ref, off_ref, hm8_ref, w13_ref, w2_ref, idx_ref, rank_ref, gate_ref, y8_ref,
                dest_v, dest_s, gate_s, row_s, gsort_s, xg, og, sem, *, tt, mch, d_model, d_exp):
    ti = pl.program_id(0)
    e = pl.program_id(1)
    chunks = d_model // LANES

    def rows_at(off):
        return pl.ds(pl.multiple_of(off, chunks), chunks)

    @pl.when(e == 0)
    def _():
        idx = idx_ref[...]
        dest = rank_ref[...]
        for ee in range(N_EXPERTS):
            dest = dest + jnp.where(idx == ee, off_ref[ti * N_EXPERTS + ee], 0)
        dest_v[...] = dest
        copies = []
        for kk in range(TOP_K):
            span = pl.ds(kk * tt, tt)
            copies.append(pltpu.make_async_copy(dest_v.at[kk], dest_s.at[span], sem.at[0]))
            copies.append(pltpu.make_async_copy(gate_ref.at[kk], gate_s.at[span], sem.at[1]))
        for cp in copies:
            cp.start()
        y8_ref[...] = jnp.zeros(y8_ref.shape, F32)
        xg[...] = jnp.zeros(xg.shape, F32)
        og[...] = jnp.zeros(og.shape, F32)
        for cp in copies:
            cp.wait()

        def invert(io, _):
            for u in range(SUBLANES):
                t = io * SUBLANES + u
                picks = [(dest_s[kk * tt + t], gate_s[kk * tt + t]) for kk in range(TOP_K)]
                for slot, gt in picks:
                    row_s[slot] = t * chunks
                    gsort_s[slot] = gt
            return 0

        lax.fori_loop(0, tt // SUBLANES, invert, 0)

    c = cnt_ref[ti * N_EXPERTS + e]
    off = off_ref[ti * N_EXPERTS + e]
    nch = (c + mch - 1) // mch
    g_unroll = 16
    s_unroll = 8

    def chunk(ch, carry):
        base = off + ch * mch
        nvalid = jnp.minimum(c - ch * mch, mch)

        def gather_row(ii):
            xg[rows_at(ii * chunks), :] = hm8_ref[rows_at(row_s[base + ii]), :]

        def gather_group(io, _):
            for u in range(g_unroll):
                gather_row(io * g_unroll + u)
            return 0

        def gather_one(ii, _):
            gather_row(ii)
            return 0

        n_full = nvalid // g_unroll
        lax.fori_loop(0, n_full, gather_group, 0)
        lax.fori_loop(n_full * g_unroll, nvalid, gather_one, 0)
        xb = jnp.concatenate([xg[pl.ds(s, mch, stride=chunks), :] for s in range(chunks)],
                             axis=-1).astype(BF16)
        h13 = jnp.dot(xb, w13_ref[...], preferred_element_type=F32)
        h1 = h13[:, :d_exp]
        hdn = (h1 * _sigmoid(h1)) * h13[:, d_exp:]
        o = jnp.dot(hdn.astype(BF16), w2_ref[...], preferred_element_type=F32)
        for s in range(chunks):
            og[pl.ds(s, mch, stride=chunks), :] = o[:, s * LANES:(s + 1) * LANES]

        def scatter_rows(first, n):
            vals = []
            for u in range(n):
                ii = first + u
                r = row_s[base + ii]
                gt = gsort_s[base + ii]
                vals.append((r, y8_ref[rows_at(r), :] + gt * og[rows_at(ii * chunks), :]))
            for r, v in vals:
                y8_ref[rows_at(r), :] = v

        def scatter_group(io, _):
            scatter_rows(io * s_unroll, s_unroll)
            return 0

        def scatter_one(ii, _):
            scatter_rows(ii, 1)
            return 0

        n_full_s = nvalid // s_unroll
        lax.fori_loop(0, n_full_s, scatter_group, 0)
        lax.fori_loop(n_full_s * s_unroll, nvalid, scatter_one, 0)
        return carry

    lax.fori_loop(0, nch, chunk, 0)


def _moe(hm8, w13, w2, idx, rank, gate, cnt_flat, off_flat, tt, mch):
    T8, _ = hm8.shape
    T = T8 // SUBLANES
    n_tiles = T // tt
    d_model = w13.shape[1]
    d_exp = w2.shape[1]
    assert d_model == SUBLANES * LANES and tt & (tt - 1) == 0 and mch % SUBLANES == 0
    kern = functools.partial(_moe_kernel, tt=tt, mch=mch, d_model=d_model, d_exp=d_exp)
    single = dict(pipeline_mode=pl.Buffered(1))
    picks = pl.BlockSpec((TOP_K, tt), lambda t, e, c, o: (0, t))
    grid_spec = pltpu.PrefetchScalarGridSpec(
        num_scalar_prefetch=2,
        grid=(n_tiles, N_EXPERTS),
        in_specs=[pl.BlockSpec((tt * SUBLANES, LANES), lambda t, e, c, o: (t, 0), **single),
                  pl.BlockSpec((None, d_model, 2 * d_exp), lambda t, e, c, o: (e, 0, 0)),
                  pl.BlockSpec((None, d_exp, d_model), lambda t, e, c, o: (e, 0, 0)),
                  picks, picks, picks],
        out_specs=pl.BlockSpec((tt * SUBLANES, LANES), lambda t, e, c, o: (t, 0), **single),
        scratch_shapes=[pltpu.VMEM((TOP_K, tt), jnp.int32),
                        pltpu.SMEM((tt * TOP_K,), jnp.int32), pltpu.SMEM((tt * TOP_K,), F32),
                        pltpu.SMEM((tt * TOP_K,), jnp.int32), pltpu.SMEM((tt * TOP_K,), F32),
                        pltpu.VMEM((mch * SUBLANES, LANES), F32),
                        pltpu.VMEM((mch * SUBLANES, LANES), F32),
                        pltpu.SemaphoreType.DMA((2,))],
    )
    return pl.pallas_call(
        kern,
        grid_spec=grid_spec,
        out_shape=jax.ShapeDtypeStruct((T8, LANES), F32),
        compiler_params=_cparams(("arbitrary", "arbitrary")),
        name="moe_experts",
    )(cnt_flat, off_flat, hm8, w13, w2, idx, rank, gate)


def _final_kernel(y8_ref, x1_ref, hmb_ref, ws13_ref, ws2_ref, gp_ref, wg_ref, p_ref, wp_ref, o_ref,
                  *, tm, d_sh):
    chunks = x1_ref.shape[1] // LANES
    routed = jnp.concatenate([y8_ref[pl.ds(s, tm, stride=chunks), :] for s in range(chunks)], axis=-1)
    hmb = hmb_ref[...]
    h13 = jnp.dot(hmb, ws13_ref[...], preferred_element_type=F32)
    h1 = h13[:, :d_sh]
    hdn = (h1 * _sigmoid(h1)) * h13[:, d_sh:]
    shared = jnp.dot(hdn.astype(BF16), ws2_ref[...], preferred_element_type=F32)
    x2 = x1_ref[...] + (routed + shared)
    ms = jnp.mean(x2 * x2, axis=-1, keepdims=True)
    hp = (x2 * lax.rsqrt(ms + EPS) * gp_ref[...]).astype(BF16)
    gate = _sigmoid(jnp.dot(hp, wg_ref[...], preferred_element_type=F32))
    proj = jnp.dot(p_ref[...].astype(BF16), wp_ref[...], preferred_element_type=F32)
    o_ref[...] = x2 + gate * proj


def _final(y8, x1, hmb, ws13, ws2, gp, wg, p2, wp):
    T, D = x1.shape
    tm = min(256, T)
    d_sh = ws2.shape[0]
    kern = functools.partial(_final_kernel, tm=tm, d_sh=d_sh)
    row = lambda w: pl.BlockSpec((tm, w), lambda i: (i, 0))
    return pl.pallas_call(
        kern,
        grid=(T // tm,),
        in_specs=[pl.BlockSpec((tm * SUBLANES, LANES), lambda i: (i, 0)), row(D), row(D),
                  _full(ws13.shape), _full(ws2.shape), _full(gp.shape), _full(wg.shape),
                  row(p2.shape[1]), _full(wp.shape)],
        out_specs=row(D),
        out_shape=jax.ShapeDtypeStruct((T, D), F32),
        compiler_params=_cparams(("parallel",)),
        name="final",
    )(y8, x1, hmb, ws13, ws2, gp, wg, p2, wp)


def _block_diag(w):
    nh, d, _ = w.shape
    eye = jnp.eye(nh, dtype=w.dtype)
    return (eye[:, None, :, None] * w[:, :, None, :]).reshape(nh * d, nh * d)


def _prep_weights(i, g_norm1, w_in, g_q, g_k, lam_q1, lam_k1, lam_q2, lam_k2, g_subln, conv_w, conv_b,
                  w_a, b_a, w_x, b_x, lam_rg, w_out, g_norm2, w_router, router_bias, w_e1, w_e3, w_e2,
                  w_s1, w_s3, w_s2, g_ple, w_ple_gate, w_ple_proj):
    return dict(
        g1=g_norm1[i][None], w_in=w_in[i].astype(BF16),
        wqt=w_in[i][:, :ATT_W].T.astype(BF16), wkt=w_in[i][:, ATT_W:2 * ATT_W].T.astype(BF16),
        wvt=w_in[i][:, 2 * ATT_W:3 * ATT_W].T.astype(BF16),
        gq_col=g_q[i][:, None], gk_col=g_k[i][:, None],
        lq1=lam_q1[i][None], lk1=lam_k1[i][None], lq2=lam_q2[i][None], lk2=lam_k2[i][None],
        gs=g_subln[i][None], gs_col=g_subln[i][:, None],
        cw=conv_w[i], cb=conv_b[i][None],
        wa=_block_diag(w_a[i]).astype(BF16), ba=b_a[i][None],
        wx=_block_diag(w_x[i]).astype(BF16), bx=b_x[i][None], lam_rg=lam_rg[i][None],
        wo=w_out[i].astype(BF16), g2=g_norm2[i][None],
        wr_t=w_router[i].T.astype(BF16), rb=router_bias[i][:, None],
        w13=jnp.concatenate([w_e1[i], w_e3[i]], axis=-1).astype(BF16), w2=w_e2[i].astype(BF16),
        ws13=jnp.concatenate([w_s1[i], w_s3[i]], axis=-1).astype(BF16), ws2=w_s2[i].astype(BF16),
        gp=g_ple[i][None], wg=w_ple_gate[i].astype(BF16), wp=w_ple_proj[i].astype(BF16),
    )


def _moe_stage(att, rg, x2, p2, W, tt, mch):
    T = x2.shape[0]
    x1, hm8, hmb, idx, gate, rank, cnt = _outproj_router(att, rg, x2, W['wo'], W['g2'], W['wr_t'],
                                                         W['rb'], tt)
    cnt_i = cnt[:, :, 0].astype(jnp.int32)
    off_i = jnp.cumsum(cnt_i, axis=1) - cnt_i
    y8 = _moe(hm8, W['w13'], W['w2'], idx, rank, gate, cnt_i.reshape(-1), off_i.reshape(-1), tt, mch)
    return _final(y8, x1, hmb, W['ws13'], W['ws2'], W['gp'], W['wg'], p2, W['wp'])


def _slopes_col():
    h = jnp.arange(N_MAPS) // 2 + 1
    return (LOG2E * jnp.exp2(-8.0 * h.astype(F32) / N_HEADS))[:, None]


def kernel(x_prompt, x_sample, cache_k, cache_v, state_conv, state_h, page_table, p_prompt, p_sample, g_norm1, w_in, g_q, g_k, lam_q1, lam_k1, lam_q2, lam_k2, g_subln, conv_w, conv_b, w_a, b_a, w_x, b_x, lam_rg, w_out, g_norm2, w_router, router_bias, w_e1, w_e3, w_e2, w_s1, w_s3, w_s2, g_ple, w_ple_gate, w_ple_proj):
    depth = w_in.shape[0]
    B, S, D = x_prompt.shape
    NB, NS, _ = x_sample.shape
    assert NS == 1
    n_pool = cache_k.shape[1]
    yp = x_prompt.reshape(B * S, D)
    ys = x_sample.reshape(NB * NS, D)
    kp_l, vp_l, cp_l, hp_l, ks_l, vs_l, cs_l, hs_l = [], [], [], [], [], [], [], []
    for i in range(depth):
        W = _prep_weights(i, g_norm1, w_in, g_q, g_k, lam_q1, lam_k1, lam_q2, lam_k2, g_subln, conv_w,
                          conv_b, w_a, b_a, w_x, b_x, lam_rg, w_out, g_norm2, w_router, router_bias,
                          w_e1, w_e3, w_e2, w_s1, w_s3, w_s2, g_ple, w_ple_gate, w_ple_proj)
        lam_init = 0.8 - 0.6 * math.exp(-0.3 * i)
        lam_vecs = (W['lq1'], W['lk1'], W['lq2'], W['lk2'])
        inproj_w = (W['g1'], W['w_in'], W['wqt'], W['wkt'], W['wvt'], W['gq_col'], W['gk_col'])

        tb = min(512, S)
        qt, kt, kbe, v4, vte, xr, gr = _inproj(yp, B, tb, *inproj_w)
        att = _prompt_attention(qt, kbe, vte, *lam_vecs, W['gs_col'], B, S, tb, lam_init)
        rg, ctail, htail = _rglru_seq(xr, gr, W['cw'], W['cb'], W['wa'], W['ba'], W['wx'], W['bx'],
                                      W['lam_rg'], B, S)
        tt_p = min(4096, B * S)
        mch_p = tt_p // SUBLANES + tt_p // 64
        yp = _moe_stage(att, rg, yp, p_prompt[i].reshape(B * S, -1), W, tt_p, mch_p)
        kp_l.append(jnp.transpose(kt.reshape(B, N_HEADS, 2, HEAD_DIM, S), (0, 4, 1, 2, 3)))
        vp_l.append(v4.reshape(B, S, N_HEADS, V_DIM))
        cp_l.append(ctail[:, SUBLANES - (CONV_W - 1):])
        hp_l.append(htail[:, SUBLANES - 1])

        qt, kt, _, v4, _, xr, gr = _inproj(ys, 1, PAGE, *inproj_w)
        k = kt[0].T
        v = v4.reshape(NB, N_HEADS, V_DIM)
        qm = qt.T.reshape(NB, N_MAPS, V_DIM)[:, :, :HEAD_DIM]
        msel = jnp.eye(N_MAPS, dtype=BF16)
        qblk = (qm[:, :, None, :] * msel[None, :, :, None]).reshape(NB, N_MAPS, ATT_W)
        v_rows = jnp.repeat(v, 2, axis=1)
        cache_kt = jnp.transpose(cache_k[i], (0, 2, 3, 4, 1)).reshape(n_pool, ATT_W, PAGE)
        cache_v2 = cache_v[i].reshape(n_pool, PAGE * N_HEADS, V_DIM)
        att_s = _decode_attention(qblk, k.reshape(NB, 1, ATT_W), v_rows, _slopes_col(), *lam_vecs,
                                  W['gs'], cache_kt, cache_v2, page_table, lam_init)
        rg_s, h_s = _rglru_step(xr, gr, state_conv[i], state_h[i], W['cw'], W['cb'], W['wa'], W['ba'],
                                W['wx'], W['bx'], W['lam_rg'])
        ys = _moe_stage(att_s.reshape(NB, ATT_W).astype(BF16), rg_s, ys,
                        p_sample[i].reshape(NB * NS, -1), W, NB, 32)
        ks_l.append(k.reshape(NB, NS, N_HEADS, 2, HEAD_DIM))
        vs_l.append(v.reshape(NB, NS, N_HEADS, V_DIM))
        cs_l.append(jnp.concatenate([state_conv[i][:, 1:], xr[:, None, :]], axis=1))
        hs_l.append(h_s)
    return (yp.reshape(B, S, D), ys.reshape(NB, NS, D), jnp.stack(kp_l), jnp.stack(vp_l),
            jnp.stack(cp_l), jnp.stack(hp_l), jnp.stack(ks_l), jnp.stack(vs_l), jnp.stack(cs_l),
            jnp.stack(hs_l))
```

```python
import functools
import math

import jax
import jax.numpy as jnp
from jax import lax
from jax.experimental import pallas as pl
from jax.experimental.pallas import tpu as pltpu

F32 = jnp.float32
BF16 = jnp.bfloat16

EPS = 1e-6
NEG_INF = -1e30
LANES = 128
SUBLANES = 8
N_HEADS = 4
HEAD_DIM = 64
V_DIM = 2 * HEAD_DIM
N_MAPS = 2 * N_HEADS
ATT_W = N_HEADS * V_DIM
RG_W = 512
RG_C = 8.0
CONV_W = 4
N_EXPERTS = 64
N_GROUPS = 8
GROUP_SIZE = N_EXPERTS // N_GROUPS
TOPK_GROUPS = 4
TOP_K = 8
ROUTE_SCALE = 2.5
ATT_SCALE = HEAD_DIM ** -0.5
LOG2E = math.log2(math.e)
VE_ROWS = V_DIM + 16
PAGE = 128
VMEM_LIMIT = 56 * 1024 * 1024


def _cparams(sem):
    return pltpu.CompilerParams(dimension_semantics=sem, vmem_limit_bytes=VMEM_LIMIT)


def _sigmoid(x):
    return 1.0 / (1.0 + jnp.exp(-x))


def _nt_dot(a, b):
    return lax.dot_general(a, b, (((1,), (1,)), ((), ())), preferred_element_type=F32)


def _full(shape):
    n = len(shape)
    return pl.BlockSpec(shape, lambda *_: (0,) * n)


def _alibi_slopes2():
    return [LOG2E * 2.0 ** (-8.0 * (hh + 1) / N_HEADS) for hh in range(N_HEADS)]


def _inproj_kernel(x_ref, g1_ref, w_ref, wqt_ref, wkt_ref, wvt_ref, gq_ref, gk_ref,
                   qt_ref, kt_ref, kbe_ref, v4_ref, vte_ref, xr_ref, gr_ref, *, tb):
    x = x_ref[...]
    tm = x.shape[0]
    ms = jnp.mean(x * x, axis=-1, keepdims=True)
    h = (x * lax.rsqrt(ms + EPS) * g1_ref[...]).astype(BF16)

    def proj(lo, hi):
        return jnp.dot(h, w_ref[:, lo:hi], preferred_element_type=F32)

    def feature_major_normed(wt_ref, g_ref):
        z = _nt_dot(wt_ref[...], h).reshape(N_MAPS, HEAD_DIM, tm)
        zms = jnp.mean(z * z, axis=1, keepdims=True)
        return z * lax.rsqrt(zms + EPS) * g_ref[...][None]

    qn = feature_major_normed(wqt_ref, gq_ref) * (ATT_SCALE * LOG2E)
    kn = feature_major_normed(wkt_ref, gk_ref)
    kt_ref[...] = kn.reshape(N_MAPS * HEAD_DIM, tm)

    row8 = lax.broadcasted_iota(jnp.int32, (SUBLANES, tm), 0)
    pad = jnp.zeros((V_DIM - HEAD_DIM - SUBLANES, tm), F32)
    q_tail = jnp.concatenate([jnp.where(row8 < 3, 1.0, 0.0), pad], axis=0).astype(BF16)
    tok = pl.program_id(0) * tm + lax.broadcasted_iota(jnp.int32, (1, tm), 1)
    k_local = (tok % tb).astype(F32)
    slopes2 = _alibi_slopes2()
    kte = []
    for mm in range(N_MAPS):
        qt_ref[mm * V_DIM:mm * V_DIM + HEAD_DIM, :] = qn[mm].astype(BF16)
        qt_ref[mm * V_DIM + HEAD_DIM:(mm + 1) * V_DIM, :] = q_tail
        val = slopes2[mm // 2] * k_local
        hi = val.astype(BF16).astype(F32)
        mid = (val - hi).astype(BF16).astype(F32)
        lo = ((val - hi) - mid).astype(BF16).astype(F32)
        k_tail = jnp.where(row8 == 0, hi, jnp.where(row8 == 1, mid, jnp.where(row8 == 2, lo, 0.0)))
        kte += [kn[mm], k_tail, pad]
    kbe_ref[...] = jnp.concatenate(kte, axis=0).T.astype(BF16)

    v = proj(2 * ATT_W, 3 * ATT_W)
    for hh in range(N_HEADS):
        v4_ref[pl.ds(hh, tm, stride=N_HEADS), :] = v[:, hh * V_DIM:(hh + 1) * V_DIM]
    vt = _nt_dot(wvt_ref[...], h).astype(BF16)
    ones = jnp.ones((VE_ROWS - V_DIM, tm), BF16)
    for hh in range(N_HEADS):
        vte_ref[hh * VE_ROWS:hh * VE_ROWS + V_DIM, :] = vt[hh * V_DIM:(hh + 1) * V_DIM]
        vte_ref[hh * VE_ROWS + V_DIM:(hh + 1) * VE_ROWS, :] = ones
    xr_ref[...] = proj(3 * ATT_W, 3 * ATT_W + RG_W)
    gr_ref[...] = proj(3 * ATT_W + RG_W, 3 * ATT_W + 2 * RG_W)


def _inproj(x2, batch, tb, g1, w_in_bf, wqt, wkt, wvt, gq_col, gk_col):
    T, D = x2.shape
    seq = T // batch
    tm = min(512, seq)
    tiles = seq // tm
    row = lambda w: pl.BlockSpec((tm, w), lambda i: (i, 0))
    col = lambda r: pl.BlockSpec((None, r, tm), lambda i: (i, 0, 0))
    out_shape = (
        jax.ShapeDtypeStruct((T // tm, N_MAPS * V_DIM, tm), BF16),
        jax.ShapeDtypeStruct((batch, ATT_W, seq), F32),
        jax.ShapeDtypeStruct((T, N_MAPS * V_DIM), BF16),
        jax.ShapeDtypeStruct((T * N_HEADS, V_DIM), F32),
        jax.ShapeDtypeStruct((T // tm, N_HEADS * VE_ROWS, tm), BF16),
        jax.ShapeDtypeStruct((T, RG_W), F32), jax.ShapeDtypeStruct((T, RG_W), F32),
    )
    args = (x2, g1, w_in_bf, wqt, wkt, wvt, gq_col, gk_col)
    return pl.pallas_call(
        functools.partial(_inproj_kernel, tb=tb),
        grid=(T // tm,),
        in_specs=[row(D)] + [_full(a.shape) for a in args[1:]],
        out_specs=(col(N_MAPS * V_DIM),
                   pl.BlockSpec((None, ATT_W, tm), lambda i: (i // tiles, 0, i % tiles)),
                   row(N_MAPS * V_DIM),
                   pl.BlockSpec((tm * N_HEADS, V_DIM), lambda i: (i, 0)),
                   col(N_HEADS * VE_ROWS), row(RG_W), row(RG_W)),
        out_shape=out_shape,
        compiler_params=_cparams(("parallel",)),
        name="inproj",
    )(*args)


def _lam_value(lq1, lk1, lq2, lk2, lam_init):
    s1 = jnp.sum(lq1 * lk1, axis=-1, keepdims=True)
    s2 = jnp.sum(lq2 * lk2, axis=-1, keepdims=True)
    return jnp.exp(s1) - jnp.exp(s2) + lam_init


def _subln(o, g, lam_init):
    ms = jnp.mean(o * o, axis=-1, keepdims=True)
    return (o * lax.rsqrt(ms + EPS) * g) * (1.0 - lam_init)


def _prompt_attn_kernel(it_ref, jt_ref, qt_ref, kbe_ref, vte_ref, lq1_ref, lk1_ref, lq2_ref, lk2_ref,
                        gsc_ref, o_ref, m_scr, acc_scr, mask_scr, *, tb, strip, lam_init):
    pidx = pl.program_id(1)
    i = it_ref[pidx]
    j = jt_ref[pidx]
    slopes2 = _alibi_slopes2()

    @pl.when(pidx == 0)
    def _():
        kl = lax.broadcasted_iota(jnp.int32, (tb, tb), 0)
        ql = lax.broadcasted_iota(jnp.int32, (tb, tb), 1)
        mask_scr[...] = jnp.where(kl <= ql, 0.0, NEG_INF)

    @pl.when(j == 0)
    def _():
        m_scr[...] = jnp.full(m_scr.shape, NEG_INF, F32)
        acc_scr[...] = jnp.zeros(acc_scr.shape, F32)

    gap = ((i - j) * tb).astype(F32)

    def update(diagonal):
        for mm in range(N_MAPS):
            hh = mm // 2
            off = -slopes2[hh] * gap
            for s0 in range(0, tb, strip):
                nk = s0 + strip if diagonal else tb
                lanes = slice(s0, s0 + strip)
                u = jnp.dot(kbe_ref[:nk, mm * V_DIM:(mm + 1) * V_DIM],
                            qt_ref[mm * V_DIM:(mm + 1) * V_DIM, lanes], preferred_element_type=F32)
                if diagonal:
                    u = u + mask_scr[:nk, lanes]
                m_prev = m_scr[mm, :, lanes]
                m_new = jnp.maximum(m_prev, jnp.max(u, axis=0, keepdims=True) + off)
                alpha = jnp.exp2(m_prev - m_new)
                p = jnp.exp2(u - (m_new - off)).astype(BF16)
                acc_scr[mm, :, lanes] = alpha * acc_scr[mm, :, lanes] + jnp.dot(
                    vte_ref[hh * VE_ROWS:(hh + 1) * VE_ROWS, :nk], p, preferred_element_type=F32)
                m_scr[mm, :, lanes] = m_new

    @pl.when(j < i)
    def _():
        update(False)

    @pl.when(j == i)
    def _():
        update(True)
        lam = _lam_value(lq1_ref[...], lk1_ref[...], lq2_ref[...], lk2_ref[...], lam_init)
        for hh in range(N_HEADS):
            a1 = acc_scr[2 * hh]
            a2 = acc_scr[2 * hh + 1]
            o1 = a1[:V_DIM] / a1[V_DIM:V_DIM + 1]
            o2 = a2[:V_DIM] / a2[V_DIM:V_DIM + 1]
            o = o1 - lam * o2
            ms = jnp.mean(o * o, axis=0, keepdims=True)
            y = (o * lax.rsqrt(ms + EPS) * gsc_ref[...]) * (1.0 - lam_init)
            o_ref[:, hh * V_DIM:(hh + 1) * V_DIM] = y.T.astype(o_ref.dtype)


def _prompt_attention(qt, kbe, vte, lq1, lk1, lq2, lk2, gs_col, batch, seq, tb, lam_init):
    nb = seq // tb
    strip = tb
    pairs = [(i, j) for i in range(nb) for j in range(i + 1)]
    i_tab = jnp.asarray([p[0] for p in pairs], jnp.int32)
    j_tab = jnp.asarray([p[1] for p in pairs], jnp.int32)
    cst = lambda a: pl.BlockSpec(a.shape, lambda b, p, it, jt: (0,) * a.ndim)
    kern = functools.partial(_prompt_attn_kernel, tb=tb, strip=strip, lam_init=lam_init)
    grid_spec = pltpu.PrefetchScalarGridSpec(
        num_scalar_prefetch=2,
        grid=(batch, len(pairs)),
        in_specs=[pl.BlockSpec((None, N_MAPS * V_DIM, tb), lambda b, p, it, jt: (b * nb + it[p], 0, 0)),
                  pl.BlockSpec((tb, N_MAPS * V_DIM), lambda b, p, it, jt: (b * nb + jt[p], 0)),
                  pl.BlockSpec((None, N_HEADS * VE_ROWS, tb), lambda b, p, it, jt: (b * nb + jt[p], 0, 0)),
                  cst(lq1), cst(lk1), cst(lq2), cst(lk2), cst(gs_col)],
        out_specs=pl.BlockSpec((tb, ATT_W), lambda b, p, it, jt: (b * nb + it[p], 0)),
        scratch_shapes=[pltpu.VMEM((N_MAPS, 1, tb), F32),
                        pltpu.VMEM((N_MAPS, VE_ROWS, tb), F32),
                        pltpu.VMEM((tb, tb), F32)],
    )
    return pl.pallas_call(
        kern,
        grid_spec=grid_spec,
        out_shape=jax.ShapeDtypeStruct((batch * seq, ATT_W), BF16),
        compiler_params=_cparams(("arbitrary", "arbitrary")),
        name="prompt_attn",
    )(i_tab, j_tab, qt, kbe, vte, lq1, lk1, lq2, lk2, gs_col)


def _decode_attn_kernel(pt_ref, qb_ref, kn_ref, vn_ref, sl_ref, lq1_ref, lk1_ref, lq2_ref, lk2_ref,
                        gs_ref, ck_hbm, cv_hbm, o_ref, kbuf, vbuf, sems, m_scr, l_scr, acc_scr,
                        *, pps, ng, past, lam_init):
    s = pl.program_id(0)
    n_steps = pl.num_programs(0)
    g = s % ng
    slot = s % 2

    def page_copies(step, sl):
        cps = []
        for pi in range(pps):
            page = pt_ref[step * pps + pi]
            cps.append(pltpu.make_async_copy(ck_hbm.at[page], kbuf.at[sl, pi], sems.at[0, sl]))
            cps.append(pltpu.make_async_copy(cv_hbm.at[page], vbuf.at[sl, pi], sems.at[1, sl]))
        return cps

    @pl.when(s == 0)
    def _():
        for cp in page_copies(0, 0):
            cp.start()

    @pl.when(s + 1 < n_steps)
    def _():
        for cp in page_copies(s + 1, 1 - slot):
            cp.start()

    for cp in page_copies(s, slot):
        cp.wait()
    kp_refs = [kbuf.at[slot, pi] for pi in range(pps)]
    vp_refs = [vbuf.at[slot, pi] for pi in range(pps)]

    @pl.when(g == 0)
    def _():
        m_scr[...] = jnp.full(m_scr.shape, NEG_INF, F32)
        l_scr[...] = jnp.zeros(l_scr.shape, F32)
        acc_scr[...] = jnp.zeros(acc_scr.shape, F32)

    q = qb_ref[...]
    slopes = sl_ref[...]
    lane = lax.broadcasted_iota(jnp.int32, (1, PAGE), 1)
    head_of_row = lax.broadcasted_iota(jnp.int32, (N_MAPS, 1), 0) // 2
    m_run = m_scr[...]
    l_run = l_scr[...]
    acc = acc_scr[...]
    scores = []
    for pi in range(pps):
        kpos = (g * pps + pi) * PAGE + lane
        dist = (past - kpos).astype(F32)
        sc = jnp.dot(q, kp_refs[pi][...].astype(BF16), preferred_element_type=F32)
        scores.append(sc - slopes * dist)
    smax = scores[0]
    for sc in scores[1:]:
        smax = jnp.maximum(smax, sc)
    m_new = jnp.maximum(m_run, jnp.max(smax, axis=-1, keepdims=True))
    alpha = jnp.exp2(m_run - m_new)
    psum = jnp.zeros((N_MAPS, PAGE), F32)
    pvs = [jnp.zeros((N_MAPS, V_DIM), F32) for _ in range(N_HEADS)]
    for pi in range(pps):
        p = jnp.exp2(scores[pi] - m_new)
        psum = psum + p
        pb = p.astype(BF16)
        for hh in range(N_HEADS):
            vh = vp_refs[pi][pl.ds(hh, PAGE, stride=N_HEADS), :].astype(BF16)
            pvs[hh] = pvs[hh] + jnp.dot(pb, vh, preferred_element_type=F32)
    pv = pvs[0]
    for hh in range(1, N_HEADS):
        pv = jnp.where(head_of_row == hh, pvs[hh], pv)
    m_scr[...] = m_new
    l_run = alpha * l_run + jnp.sum(psum, axis=-1, keepdims=True)
    acc = alpha * acc + pv
    l_scr[...] = l_run
    acc_scr[...] = acc
    m_run = m_new

    @pl.when(g == ng - 1)
    def _():
        kn = kn_ref[...].astype(BF16).astype(F32)
        vn = vn_ref[...].astype(BF16).astype(F32)
        s = jnp.sum(q.astype(F32) * kn, axis=-1, keepdims=True)
        m_new = jnp.maximum(m_run, s)
        alpha = jnp.exp2(m_run - m_new)
        p = jnp.exp2(s - m_new)
        l_fin = alpha * l_run + p
        acc_fin = alpha * acc + p.astype(BF16).astype(F32) * vn
        lam = _lam_value(lq1_ref[...], lk1_ref[...], lq2_ref[...], lk2_ref[...], lam_init)
        outn = acc_fin / l_fin
        for hh in range(N_HEADS):
            o = outn[2 * hh:2 * hh + 1] - lam * outn[2 * hh + 1:2 * hh + 2]
            o_ref[:, hh * V_DIM:(hh + 1) * V_DIM] = _subln(o, gs_ref[...], lam_init)


def _decode_attention(qblk, k_new, v_rows, slopes, lq1, lk1, lq2, lk2, gs, cache_kt, cache_v2,
                      page_table, lam_init):
    nb, n_pages = page_table.shape
    pps = 32
    while n_pages % pps:
        pps //= 2
    ng = n_pages // pps
    past = n_pages * PAGE
    pt_flat = page_table.reshape(-1)
    cst = lambda shape: pl.BlockSpec(shape, lambda s, pt: (0,) * len(shape))
    per_seq = lambda r, w: pl.BlockSpec((None, r, w), lambda s, pt: (s // ng, 0, 0))
    kern = functools.partial(_decode_attn_kernel, pps=pps, ng=ng, past=past, lam_init=lam_init)
    grid_spec = pltpu.PrefetchScalarGridSpec(
        num_scalar_prefetch=1,
        grid=(nb * ng,),
        in_specs=[per_seq(N_MAPS, ATT_W), per_seq(1, ATT_W), per_seq(N_MAPS, V_DIM),
                  cst(slopes.shape), cst(lq1.shape), cst(lk1.shape), cst(lq2.shape), cst(lk2.shape),
                  cst(gs.shape),
                  pl.BlockSpec(memory_space=pl.ANY), pl.BlockSpec(memory_space=pl.ANY)],
        out_specs=per_seq(1, ATT_W),
        scratch_shapes=[pltpu.VMEM((2, pps, ATT_W, PAGE), F32),
                        pltpu.VMEM((2, pps, PAGE * N_HEADS, V_DIM), F32),
                        pltpu.SemaphoreType.DMA((2, 2)),
                        pltpu.VMEM((N_MAPS, 1), F32), pltpu.VMEM((N_MAPS, 1), F32),
                        pltpu.VMEM((N_MAPS, V_DIM), F32)],
    )
    return pl.pallas_call(
        kern,
        grid_spec=grid_spec,
        out_shape=jax.ShapeDtypeStruct((nb, 1, ATT_W), F32),
        compiler_params=_cparams(("arbitrary",)),
        name="decode_attn",
    )(pt_flat, qblk, k_new, v_rows, slopes, lq1, lk1, lq2, lk2, gs, cache_kt, cache_v2)


def _expm1(y):
    acc = 1.0 + y * (1.0 / 15.0)
    for n in range(14, 1, -1):
        acc = 1.0 + (y * (1.0 / n)) * acc
    return jnp.where(jnp.abs(y) < 0.25, y * acc, jnp.exp(y) - 1.0)


def _log1p(z):
    return jnp.where(z < 1e-4, z * (1.0 - z * (0.5 - z * (1.0 / 3.0))), jnp.log(1.0 + z))


def _rg_gates(xc, wa_ref, ba_ref, wx_ref, bx_ref, lam_ref):
    xcb = xc.astype(BF16)
    r = _sigmoid(jnp.dot(xcb, wa_ref[...], preferred_element_type=F32) + ba_ref[...])
    ig = _sigmoid(jnp.dot(xcb, wx_ref[...], preferred_element_type=F32) + bx_ref[...])
    nl = -lam_ref[...]
    sp = jnp.maximum(nl, 0.0) + _log1p(jnp.exp(-jnp.abs(nl)))
    log_a = -RG_C * r * sp
    a = jnp.exp(log_a)
    bt = jnp.sqrt(-_expm1(2.0 * log_a)) * (ig * xc)
    return a, bt


def _rglru_seq_kernel(xr_ref, gr_ref, cw_ref, cb_ref, wa_ref, ba_ref, wx_ref, bx_ref, lam_ref,
                      rg_ref, ctail_ref, htail_ref, prev_scr, h_scr, *, ts):
    s = pl.program_id(1)

    @pl.when(s == 0)
    def _():
        prev_scr[...] = jnp.zeros(prev_scr.shape, F32)
        h_scr[...] = jnp.zeros(h_scr.shape, F32)

    x = xr_ref[...]
    prev = prev_scr[...]
    row8 = lax.broadcasted_iota(jnp.int32, (SUBLANES, 1), 0)
    cw = cw_ref[...]
    xc = cb_ref[...] + x * cw[CONV_W - 1:CONV_W]
    for jj in range(1, CONV_W):
        xs = pltpu.roll(x, jj, axis=0)
        ps = pltpu.roll(prev, jj, axis=0)
        top = jnp.where(row8 < jj, ps, xs[:SUBLANES])
        xs = jnp.concatenate([top, xs[SUBLANES:]], axis=0)
        xc = xc + xs * cw[CONV_W - 1 - jj:CONV_W - jj]
    a, bt = _rg_gates(xc, wa_ref, ba_ref, wx_ref, bx_ref, lam_ref)
    row = lax.broadcasted_iota(jnp.int32, (ts, 1), 0)
    d = 1
    while d < ts:
        a_s = pltpu.roll(a, d, axis=0)
        b_s = pltpu.roll(bt, d, axis=0)
        keep = row >= d
        bt = jnp.where(keep, a * b_s + bt, bt)
        a = jnp.where(keep, a * a_s, a)
        d *= 2
    h = a * h_scr[...] + bt
    rg_ref[...] = (h * jax.nn.gelu(gr_ref[...])).astype(rg_ref.dtype)
    h_scr[...] = h[ts - 1:ts]
    prev_scr[...] = x[ts - SUBLANES:ts]
    ctail_ref[...] = x[ts - SUBLANES:ts]
    htail_ref[...] = h[ts - SUBLANES:ts]


def _rglru_seq(xr, gr, cw, cb, wa_bd, ba, wx_bd, bx, lam, batch, seq):
    ts = min(256, seq)
    ns = seq // ts
    kern = functools.partial(_rglru_seq_kernel, ts=ts)
    cst = lambda a: pl.BlockSpec(a.shape, lambda b, s: (0,) * a.ndim)
    tile = pl.BlockSpec((ts, RG_W), lambda b, s: (b * ns + s, 0))
    tail = pl.BlockSpec((None, SUBLANES, RG_W), lambda b, s: (b, 0, 0))
    return pl.pallas_call(
        kern,
        grid=(batch, ns),
        in_specs=[tile, tile, cst(cw), cst(cb), cst(wa_bd), cst(ba), cst(wx_bd), cst(bx), cst(lam)],
        out_specs=(tile, tail, tail),
        out_shape=(jax.ShapeDtypeStruct((batch * seq, RG_W), BF16),
                   jax.ShapeDtypeStruct((batch, SUBLANES, RG_W), F32),
                   jax.ShapeDtypeStruct((batch, SUBLANES, RG_W), F32)),
        scratch_shapes=[pltpu.VMEM((SUBLANES, RG_W), F32), pltpu.VMEM((1, RG_W), F32)],
        compiler_params=_cparams(("parallel", "arbitrary")),
        name="rglru_seq",
    )(xr, gr, cw, cb, wa_bd, ba, wx_bd, bx, lam)


def _rglru_step_kernel(xr_ref, gr_ref, c0_ref, c1_ref, c2_ref, h0_ref, cw_ref, cb_ref, wa_ref, ba_ref,
                       wx_ref, bx_ref, lam_ref, rg_ref, h_ref):
    x = xr_ref[...]
    cw = cw_ref[...]
    xc = cb_ref[...] + c0_ref[...] * cw[0:1]
    xc = xc + c1_ref[...] * cw[1:2]
    xc = xc + c2_ref[...] * cw[2:3]
    xc = xc + x * cw[3:4]
    a, bt = _rg_gates(xc, wa_ref, ba_ref, wx_ref, bx_ref, lam_ref)
    h = a * h0_ref[...] + bt
    h_ref[...] = h
    rg_ref[...] = (h * jax.nn.gelu(gr_ref[...])).astype(rg_ref.dtype)


def _rglru_step(xr, gr, conv_state, h0, cw, cb, wa_bd, ba, wx_bd, bx, lam):
    nb = xr.shape[0]
    args = (xr, gr, conv_state[:, 0], conv_state[:, 1], conv_state[:, 2], h0,
            cw, cb, wa_bd, ba, wx_bd, bx, lam)
    return pl.pallas_call(
        _rglru_step_kernel,
        grid=(1,),
        in_specs=[_full(a.shape) for a in args],
        out_specs=(_full((nb, RG_W)), _full((nb, RG_W))),
        out_shape=(jax.ShapeDtypeStruct((nb, RG_W), BF16), jax.ShapeDtypeStruct((nb, RG_W), F32)),
        compiler_params=_cparams(("arbitrary",)),
        name="rglru_step",
    )(*args)


def _outproj_router_kernel(att_ref, rg_ref, x_ref, wo_ref, g2_ref, wr_ref, rb_ref, tri_ref,
                           x1_ref, hm8_ref, hmb_ref, idx_ref, gate_ref, rank_ref, cnt_ref,
                           carry_scr, *, tm, tiles_per_super):
    i = pl.program_id(0)

    @pl.when(i % tiles_per_super == 0)
    def _():
        carry_scr[...] = jnp.zeros(carry_scr.shape, F32)

    mixed = (jnp.dot(att_ref[...], wo_ref[:ATT_W, :], preferred_element_type=F32)
             + jnp.dot(rg_ref[...], wo_ref[ATT_W:, :], preferred_element_type=F32))
    x1 = x_ref[...] + mixed
    x1_ref[...] = x1
    ms = jnp.mean(x1 * x1, axis=-1, keepdims=True)
    hm = x1 * lax.rsqrt(ms + EPS) * g2_ref[...]
    hmb = hm.astype(BF16)
    hmb_ref[...] = hmb
    for s in range(SUBLANES):
        hm8_ref[pl.ds(s, tm, stride=SUBLANES), :] = hm[:, s * LANES:(s + 1) * LANES]

    scores = _sigmoid(_nt_dot(wr_ref[...], hmb))
    choice = scores + rb_ref[...]
    ch3 = choice.reshape(N_GROUPS, GROUP_SIZE, tm)
    i8 = lax.broadcasted_iota(jnp.int32, (N_GROUPS, GROUP_SIZE, tm), 1)
    m1 = jnp.max(ch3, axis=1, keepdims=True)
    f1 = jnp.min(jnp.where(ch3 == m1, i8, GROUP_SIZE), axis=1, keepdims=True)
    m2 = jnp.max(jnp.where(i8 == f1, -jnp.inf, ch3), axis=1, keepdims=True)
    gs3 = m1 + m2
    gs = gs3.reshape(N_GROUPS, tm)
    gi = lax.broadcasted_iota(jnp.int32, (N_GROUPS, tm), 0)
    grank = jnp.zeros((N_GROUPS, tm), jnp.int32)
    for g2 in range(N_GROUPS):
        rowv = gs[g2:g2 + 1, :]
        beats = (rowv > gs) | ((rowv == gs) & (g2 < gi))
        grank = grank + beats.astype(jnp.int32)
    gsel = (grank < TOPK_GROUPS).reshape(N_GROUPS, 1, tm)
    masked = jnp.where(gsel, ch3, NEG_INF).reshape(N_EXPERTS, tm)

    ei = lax.broadcasted_iota(jnp.int32, (N_EXPERTS, tm), 0)
    picks, gvals = [], []
    sel = jnp.zeros((N_EXPERTS, tm), F32)
    for _ in range(TOP_K):
        mx = jnp.max(masked, axis=0, keepdims=True)
        first = jnp.min(jnp.where(masked == mx, ei, N_EXPERTS), axis=0, keepdims=True)
        hit = ei == first
        picks.append(first)
        gvals.append(jnp.sum(jnp.where(hit, scores, 0.0), axis=0, keepdims=True))
        sel = jnp.where(hit, 1.0, sel)
        masked = jnp.where(hit, -jnp.inf, masked)
    gsum = gvals[0]
    for kk in range(1, TOP_K):
        gsum = gsum + gvals[kk]

    pos = jnp.dot(sel.astype(BF16), tri_ref[...], preferred_element_type=F32) + carry_scr[...]
    for kk in range(TOP_K):
        idx_ref[kk:kk + 1, :] = picks[kk]
        gate_ref[kk:kk + 1, :] = gvals[kk] / gsum * ROUTE_SCALE
        rk = jnp.sum(jnp.where(ei == picks[kk], pos, 0.0), axis=0, keepdims=True)
        rank_ref[kk:kk + 1, :] = rk.astype(jnp.int32)
    carry = carry_scr[...] + jnp.sum(sel, axis=1, keepdims=True)
    carry_scr[...] = carry
    cnt_ref[...] = jnp.broadcast_to(carry, cnt_ref.shape)


def _outproj_router(att, rg, x2, wo_bf, g2, wr_t, rb_col, tt):
    T, D = x2.shape
    tm = min(512, T)
    tiles_per_super = tt // tm
    n_super = T // tt
    tri = jnp.triu(jnp.ones((tm, tm), BF16), k=1)
    kern = functools.partial(_outproj_router_kernel, tm=tm, tiles_per_super=tiles_per_super)
    row = lambda w: pl.BlockSpec((tm, w), lambda i: (i, 0))
    col = pl.BlockSpec((TOP_K, tm), lambda i: (0, i))
    return pl.pallas_call(
        kern,
        grid=(T // tm,),
        in_specs=[row(ATT_W), row(RG_W), row(D), _full(wo_bf.shape), _full(g2.shape),
                  _full(wr_t.shape), _full(rb_col.shape), _full(tri.shape)],
        out_specs=(row(D), pl.BlockSpec((tm * SUBLANES, LANES), lambda i: (i, 0)), row(D),
                   col, col, col,
                   pl.BlockSpec((None, N_EXPERTS, LANES), lambda i: (i // tiles_per_super, 0, 0))),
        out_shape=(jax.ShapeDtypeStruct((T, D), F32),
                   jax.ShapeDtypeStruct((T * SUBLANES, LANES), F32),
                   jax.ShapeDtypeStruct((T, D), BF16),
                   jax.ShapeDtypeStruct((TOP_K, T), jnp.int32),
                   jax.ShapeDtypeStruct((TOP_K, T), F32),
                   jax.ShapeDtypeStruct((TOP_K, T), jnp.int32),
                   jax.ShapeDtypeStruct((n_super, N_EXPERTS, LANES), F32)),
        scratch_shapes=[pltpu.VMEM((N_EXPERTS, 1), F32)],
        compiler_params=_cparams(("arbitrary",)),
        name="outproj_router",
    )(att, rg, x2, wo_bf, g2, wr_t, rb_col, tri)


def _moe_kernel(cnt_ref, off_ref, hm8_ref, w13_ref, w2_ref, idx_ref, rank_ref, gate_ref, y8_ref,
                dest_v, dest_s, gate_s, row_s, gsort_s, xg, og, sem, *, tt, mch, d_model, d_exp):
    ti = pl.program_id(0)
    e = pl.program_id(1)
    chunks = d_model // LANES

    def rows_at(off):
        return pl.ds(pl.multiple_of(off, chunks), chunks)

    @pl.when(e == 0)
    def _():
        idx = idx_ref[...]
        dest = rank_ref[...]
        for ee in range(N_EXPERTS):
            dest = dest + jnp.where(idx == ee, off_ref[ti * N_EXPERTS + ee], 0)
        dest_v[...] = dest
        copies = []
        for kk in range(TOP_K):
            span = pl.ds(kk * tt, tt)
            copies.append(pltpu.make_async_copy(dest_v.at[kk], dest_s.at[span], sem.at[0]))
            copies.append(pltpu.make_async_copy(gate_ref.at[kk], gate_s.at[span], sem.at[1]))
        for cp in copies:
            cp.start()
        y8_ref[...] = jnp.zeros(y8_ref.shape, F32)
        xg[...] = jnp.zeros(xg.shape, F32)
        og[...] = jnp.zeros(og.shape, F32)
        for cp in copies:
            cp.wait()

        def invert(io, _):
            for u in range(SUBLANES):
                t = io * SUBLANES + u
                picks = [(dest_s[kk * tt + t], gate_s[kk * tt + t]) for kk in range(TOP_K)]
                for slot, gt in picks:
                    row_s[slot] = t * chunks
                    gsort_s[slot] = gt
            return 0

        lax.fori_loop(0, tt // SUBLANES, invert, 0)

    c = cnt_ref[ti * N_EXPERTS + e]
    off = off_ref[ti * N_EXPERTS + e]
    nch = (c + mch - 1) // mch
    g_unroll = 16
    s_unroll = 8

    def chunk(ch, carry):
        base = off + ch * mch
        nvalid = jnp.minimum(c - ch * mch, mch)

        def gather_row(ii):
            xg[rows_at(ii * chunks), :] = hm8_ref[rows_at(row_s[base + ii]), :]

        def gather_group(io, _):
            for u in range(g_unroll):
                gather_row(io * g_unroll + u)
            return 0

        def gather_one(ii, _):
            gather_row(ii)
            return 0

        n_full = nvalid // g_unroll
        lax.fori_loop(0, n_full, gather_group, 0)
        lax.fori_loop(n_full * g_unroll, nvalid, gather_one, 0)
        xb = jnp.concatenate([xg[pl.ds(s, mch, stride=chunks), :] for s in range(chunks)],
                             axis=-1).astype(BF16)
        h13 = jnp.dot(xb, w13_ref[...], preferred_element_type=F32)
        h1 = h13[:, :d_exp]
        hdn = (h1 * _sigmoid(h1)) * h13[:, d_exp:]
        o = jnp.dot(hdn.astype(BF16), w2_ref[...], preferred_element_type=F32)
        for s in range(chunks):
            og[pl.ds(s, mch, stride=chunks), :] = o[:, s * LANES:(s + 1) * LANES]

        def scatter_rows(first, n):
            vals = []
            for u in range(n):
                ii = first + u
                r = row_s[base + ii]
                gt = gsort_s[base + ii]
                vals.append((r, y8_ref[rows_at(r), :] + gt * og[rows_at(ii * chunks), :]))
            for r, v in vals:
                y8_ref[rows_at(r), :] = v

        def scatter_group(io, _):
            scatter_rows(io * s_unroll, s_unroll)
            return 0

        def scatter_one(ii, _):
            scatter_rows(ii, 1)
            return 0

        n_full_s = nvalid // s_unroll
        lax.fori_loop(0, n_full_s, scatter_group, 0)
        lax.fori_loop(n_full_s * s_unroll, nvalid, scatter_one, 0)
        return carry

    lax.fori_loop(0, nch, chunk, 0)


def _moe(hm8, w13, w2, idx, rank, gate, cnt_flat, off_flat, tt, mch):
    T8, _ = hm8.shape
    T = T8 // SUBLANES
    n_tiles = T // tt
    d_model = w13.shape[1]
    d_exp = w2.shape[1]
    assert d_model == SUBLANES * LANES and tt & (tt - 1) == 0 and mch % SUBLANES == 0
    kern = functools.partial(_moe_kernel, tt=tt, mch=mch, d_model=d_model, d_exp=d_exp)
    single = dict(pipeline_mode=pl.Buffered(1))
    picks = pl.BlockSpec((TOP_K, tt), lambda t, e, c, o: (0, t))
    grid_spec = pltpu.PrefetchScalarGridSpec(
        num_scalar_prefetch=2,
        grid=(n_tiles, N_EXPERTS),
        in_specs=[pl.BlockSpec((tt * SUBLANES, LANES), lambda t, e, c, o: (t, 0), **single),
                  pl.BlockSpec((None, d_model, 2 * d_exp), lambda t, e, c, o: (e, 0, 0)),
                  pl.BlockSpec((None, d_exp, d_model), lambda t, e, c, o: (e, 0, 0)),
                  picks, picks, picks],
        out_specs=pl.BlockSpec((tt * SUBLANES, LANES), lambda t, e, c, o: (t, 0), **single),
        scratch_shapes=[pltpu.VMEM((TOP_K, tt), jnp.int32),
                        pltpu.SMEM((tt * TOP_K,), jnp.int32), pltpu.SMEM((tt * TOP_K,), F32),
                        pltpu.SMEM((tt * TOP_K,), jnp.int32), pltpu.SMEM((tt * TOP_K,), F32),
                        pltpu.VMEM((mch * SUBLANES, LANES), F32),
                        pltpu.VMEM((mch * SUBLANES, LANES), F32),
                        pltpu.SemaphoreType.DMA((2,))],
    )
    return pl.pallas_call(
        kern,
        grid_spec=grid_spec,
        out_shape=jax.ShapeDtypeStruct((T8, LANES), F32),
        compiler_params=_cparams(("arbitrary", "arbitrary")),
        name="moe_experts",
    )(cnt_flat, off_flat, hm8, w13, w2, idx, rank, gate)


def _final_kernel(y8_ref, x1_ref, hmb_ref, ws13_ref, ws2_ref, gp_ref, wg_ref, p_ref, wp_ref, o_ref,
                  *, tm, d_sh):
    chunks = x1_ref.shape[1] // LANES
    routed = jnp.concatenate([y8_ref[pl.ds(s, tm, stride=chunks), :] for s in range(chunks)], axis=-1)
    hmb = hmb_ref[...]
    h13 = jnp.dot(hmb, ws13_ref[...], preferred_element_type=F32)
    h1 = h13[:, :d_sh]
    hdn = (h1 * _sigmoid(h1)) * h13[:, d_sh:]
    shared = jnp.dot(hdn.astype(BF16), ws2_ref[...], preferred_element_type=F32)
    x2 = x1_ref[...] + (routed + shared)
    ms = jnp.mean(x2 * x2, axis=-1, keepdims=True)
    hp = (x2 * lax.rsqrt(ms + EPS) * gp_ref[...]).astype(BF16)
    gate = _sigmoid(jnp.dot(hp, wg_ref[...], preferred_element_type=F32))
    proj = jnp.dot(p_ref[...].astype(BF16), wp_ref[...], preferred_element_type=F32)
    o_ref[...] = x2 + gate * proj


def _final(y8, x1, hmb, ws13, ws2, gp, wg, p2, wp):
    T, D = x1.shape
    tm = min(512, T)
    d_sh = ws2.shape[0]
    kern = functools.partial(_final_kernel, tm=tm, d_sh=d_sh)
    row = lambda w: pl.BlockSpec((tm, w), lambda i: (i, 0))
    return pl.pallas_call(
        kern,
        grid=(T // tm,),
        in_specs=[pl.BlockSpec((tm * SUBLANES, LANES), lambda i: (i, 0)), row(D), row(D),
                  _full(ws13.shape), _full(ws2.shape), _full(gp.shape), _full(wg.shape),
                  row(p2.shape[1]), _full(wp.shape)],
        out_specs=row(D),
        out_shape=jax.ShapeDtypeStruct((T, D), F32),
        compiler_params=_cparams(("parallel",)),
        name="final",
    )(y8, x1, hmb, ws13, ws2, gp, wg, p2, wp)


def _block_diag(w):
    nh, d, _ = w.shape
    eye = jnp.eye(nh, dtype=w.dtype)
    return (eye[:, None, :, None] * w[:, :, None, :]).reshape(nh * d, nh * d)


def _prep_weights(i, g_norm1, w_in, g_q, g_k, lam_q1, lam_k1, lam_q2, lam_k2, g_subln, conv_w, conv_b,
                  w_a, b_a, w_x, b_x, lam_rg, w_out, g_norm2, w_router, router_bias, w_e1, w_e3, w_e2,
                  w_s1, w_s3, w_s2, g_ple, w_ple_gate, w_ple_proj):
    return dict(
        g1=g_norm1[i][None], w_in=w_in[i].astype(BF16),
        wqt=w_in[i][:, :ATT_W].T.astype(BF16), wkt=w_in[i][:, ATT_W:2 * ATT_W].T.astype(BF16),
        wvt=w_in[i][:, 2 * ATT_W:3 * ATT_W].T.astype(BF16),
        gq_col=g_q[i][:, None], gk_col=g_k[i][:, None],
        lq1=lam_q1[i][None], lk1=lam_k1[i][None], lq2=lam_q2[i][None], lk2=lam_k2[i][None],
        gs=g_subln[i][None], gs_col=g_subln[i][:, None],
        cw=conv_w[i], cb=conv_b[i][None],
        wa=_block_diag(w_a[i]).astype(BF16), ba=b_a[i][None],
        wx=_block_diag(w_x[i]).astype(BF16), bx=b_x[i][None], lam_rg=lam_rg[i][None],
        wo=w_out[i].astype(BF16), g2=g_norm2[i][None],
        wr_t=w_router[i].T.astype(BF16), rb=router_bias[i][:, None],
        w13=jnp.concatenate([w_e1[i], w_e3[i]], axis=-1).astype(BF16), w2=w_e2[i].astype(BF16),
        ws13=jnp.concatenate([w_s1[i], w_s3[i]], axis=-1).astype(BF16), ws2=w_s2[i].astype(BF16),
        gp=g_ple[i][None], wg=w_ple_gate[i].astype(BF16), wp=w_ple_proj[i].astype(BF16),
    )


def _moe_stage(att, rg, x2, p2, W, tt, mch):
    T = x2.shape[0]
    x1, hm8, hmb, idx, gate, rank, cnt = _outproj_router(att, rg, x2, W['wo'], W['g2'], W['wr_t'],
                                                         W['rb'], tt)
    cnt_i = cnt[:, :, 0].astype(jnp.int32)
    off_i = jnp.cumsum(cnt_i, axis=1) - cnt_i
    y8 = _moe(hm8, W['w13'], W['w2'], idx, rank, gate, cnt_i.reshape(-1), off_i.reshape(-1), tt, mch)
    return _final(y8, x1, hmb, W['ws13'], W['ws2'], W['gp'], W['wg'], p2, W['wp'])


def _slopes_col():
    h = jnp.arange(N_MAPS) // 2 + 1
    return (LOG2E * jnp.exp2(-8.0 * h.astype(F32) / N_HEADS))[:, None]


def kernel(x_prompt, x_sample, cache_k, cache_v, state_conv, state_h, page_table, p_prompt, p_sample, g_norm1, w_in, g_q, g_k, lam_q1, lam_k1, lam_q2, lam_k2, g_subln, conv_w, conv_b, w_a, b_a, w_x, b_x, lam_rg, w_out, g_norm2, w_router, router_bias, w_e1, w_e3, w_e2, w_s1, w_s3, w_s2, g_ple, w_ple_gate, w_ple_proj):
    depth = w_in.shape[0]
    B, S, D = x_prompt.shape
    NB, NS, _ = x_sample.shape
    assert NS == 1
    n_pool = cache_k.shape[1]
    yp = x_prompt.reshape(B * S, D)
    ys = x_sample.reshape(NB * NS, D)
    kp_l, vp_l, cp_l, hp_l, ks_l, vs_l, cs_l, hs_l = [], [], [], [], [], [], [], []
    for i in range(depth):
        W = _prep_weights(i, g_norm1, w_in, g_q, g_k, lam_q1, lam_k1, lam_q2, lam_k2, g_subln, conv_w,
                          conv_b, w_a, b_a, w_x, b_x, lam_rg, w_out, g_norm2, w_router, router_bias,
                          w_e1, w_e3, w_e2, w_s1, w_s3, w_s2, g_ple, w_ple_gate, w_ple_proj)
        lam_init = 0.8 - 0.6 * math.exp(-0.3 * i)
        lam_vecs = (W['lq1'], W['lk1'], W['lq2'], W['lk2'])
        inproj_w = (W['g1'], W['w_in'], W['wqt'], W['wkt'], W['wvt'], W['gq_col'], W['gk_col'])

        tb = min(512, S)
        qt, kt, kbe, v4, vte, xr, gr = _inproj(yp, B, tb, *inproj_w)
        att = _prompt_attention(qt, kbe, vte, *lam_vecs, W['gs_col'], B, S, tb, lam_init)
        rg, ctail, htail = _rglru_seq(xr, gr, W['cw'], W['cb'], W['wa'], W['ba'], W['wx'], W['bx'],
                                      W['lam_rg'], B, S)
        tt_p = min(4096, B * S)
        mch_p = tt_p // SUBLANES + tt_p // 64
        yp = _moe_stage(att, rg, yp, p_prompt[i].reshape(B * S, -1), W, tt_p, mch_p)
        kp_l.append(jnp.transpose(kt.reshape(B, N_HEADS, 2, HEAD_DIM, S), (0, 4, 1, 2, 3)))
        vp_l.append(v4.reshape(B, S, N_HEADS, V_DIM))
        cp_l.append(ctail[:, SUBLANES - (CONV_W - 1):])
        hp_l.append(htail[:, SUBLANES - 1])

        qt, kt, _, v4, _, xr, gr = _inproj(ys, 1, PAGE, *inproj_w)
        k = kt[0].T
        v = v4.reshape(NB, N_HEADS, V_DIM)
        qm = qt[0].T.reshape(NB, N_MAPS, V_DIM)[:, :, :HEAD_DIM]
        msel = jnp.eye(N_MAPS, dtype=BF16)
        qblk = (qm[:, :, None, :] * msel[None, :, :, None]).reshape(NB, N_MAPS, ATT_W)
        v_rows = jnp.repeat(v, 2, axis=1)
        cache_kt = jnp.transpose(cache_k[i], (0, 2, 3, 4, 1)).reshape(n_pool, ATT_W, PAGE)
        cache_v2 = cache_v[i].reshape(n_pool, PAGE * N_HEADS, V_DIM)
        att_s = _decode_attention(qblk, k.reshape(NB, 1, ATT_W), v_rows, _slopes_col(), *lam_vecs,
                                  W['gs'], cache_kt, cache_v2, page_table, lam_init)
        rg_s, h_s = _rglru_step(xr, gr, state_conv[i], state_h[i], W['cw'], W['cb'], W['wa'], W['ba'],
                                W['wx'], W['bx'], W['lam_rg'])
        ys = _moe_stage(att_s.reshape(NB, ATT_W).astype(BF16), rg_s, ys,
                        p_sample[i].reshape(NB * NS, -1), W, NB, 32)
        ks_l.append(k.reshape(NB, NS, N_HEADS, 2, HEAD_DIM))
        vs_l.append(v.reshape(NB, NS, N_HEADS, V_DIM))
        cs_l.append(jnp.concatenate([state_conv[i][:, 1:], xr[:, None, :]], axis=1))
        hs_l.append(h_s)
    return (yp.reshape(B, S, D), ys.reshape(NB, NS, D), jnp.stack(kp_l), jnp.stack(vp_l),
            jnp.stack(cp_l), jnp.stack(hp_l), jnp.stack(ks_l), jnp.stack(vs_l), jnp.stack(cs_l),
            jnp.stack(hs_l))
```

```python
import functools
import math

import jax
import jax.numpy as jnp
from jax import lax
from jax.experimental import pallas as pl
from jax.experimental.pallas import tpu as pltpu

F32 = jnp.float32
BF16 = jnp.bfloat16

EPS = 1e-6
NEG_INF = -1e30
LANES = 128
SUBLANES = 8
N_HEADS = 4
HEAD_DIM = 64
V_DIM = 2 * HEAD_DIM
N_MAPS = 2 * N_HEADS
ATT_W = N_HEADS * V_DIM
RG_W = 512
RG_C = 8.0
CONV_W = 4
N_EXPERTS = 64
N_GROUPS = 8
GROUP_SIZE = N_EXPERTS // N_GROUPS
TOPK_GROUPS = 4
TOP_K = 8
ROUTE_SCALE = 2.5
ATT_SCALE = HEAD_DIM ** -0.5
LOG2E = math.log2(math.e)
VE_ROWS = V_DIM + 16
PAGE = 128
VMEM_LIMIT = 56 * 1024 * 1024


def _cparams(sem):
    return pltpu.CompilerParams(dimension_semantics=sem, vmem_limit_bytes=VMEM_LIMIT)


def _sigmoid(x):
    return 1.0 / (1.0 + jnp.exp(-x))


def _nt_dot(a, b):
    return lax.dot_general(a, b, (((1,), (1,)), ((), ())), preferred_element_type=F32)


def _full(shape):
    n = len(shape)
    return pl.BlockSpec(shape, lambda *_: (0,) * n)


def _alibi_slopes2():
    return [LOG2E * 2.0 ** (-8.0 * (hh + 1) / N_HEADS) for hh in range(N_HEADS)]


def _inproj_kernel(x_ref, g1_ref, w_ref, wqt_ref, wkt_ref, wvt_ref, gq_ref, gk_ref,
                   qt_ref, kt_ref, kbe_ref, v4_ref, vte_ref, xr_ref, gr_ref, *, tb):
    x = x_ref[...]
    tm = x.shape[0]
    ms = jnp.mean(x * x, axis=-1, keepdims=True)
    h = (x * lax.rsqrt(ms + EPS) * g1_ref[...]).astype(BF16)

    def proj(lo, hi):
        return jnp.dot(h, w_ref[:, lo:hi], preferred_element_type=F32)

    def feature_major_normed(wt_ref, g_ref):
        z = _nt_dot(wt_ref[...], h).reshape(N_MAPS, HEAD_DIM, tm)
        zms = jnp.mean(z * z, axis=1, keepdims=True)
        return z * lax.rsqrt(zms + EPS) * g_ref[...][None]

    qn = feature_major_normed(wqt_ref, gq_ref) * (ATT_SCALE * LOG2E)
    kn = feature_major_normed(wkt_ref, gk_ref)
    kt_ref[...] = kn.reshape(N_MAPS * HEAD_DIM, tm)

    row8 = lax.broadcasted_iota(jnp.int32, (SUBLANES, tm), 0)
    pad = jnp.zeros((V_DIM - HEAD_DIM - SUBLANES, tm), F32)
    q_tail = jnp.concatenate([jnp.where(row8 < 3, 1.0, 0.0), pad], axis=0).astype(BF16)
    tok = pl.program_id(0) * tm + lax.broadcasted_iota(jnp.int32, (1, tm), 1)
    k_local = (tok % tb).astype(F32)
    slopes2 = _alibi_slopes2()
    kte = []
    for mm in range(N_MAPS):
        qt_ref[mm * V_DIM:mm * V_DIM + HEAD_DIM, :] = qn[mm].astype(BF16)
        qt_ref[mm * V_DIM + HEAD_DIM:(mm + 1) * V_DIM, :] = q_tail
        val = slopes2[mm // 2] * k_local
        hi = val.astype(BF16).astype(F32)
        mid = (val - hi).astype(BF16).astype(F32)
        lo = ((val - hi) - mid).astype(BF16).astype(F32)
        k_tail = jnp.where(row8 == 0, hi, jnp.where(row8 == 1, mid, jnp.where(row8 == 2, lo, 0.0)))
        kte += [kn[mm], k_tail, pad]
    kbe_ref[...] = jnp.concatenate(kte, axis=0).T.astype(BF16)

    vt = _nt_dot(wvt_ref[...], h)
    ones = jnp.ones((VE_ROWS - V_DIM, tm), BF16)
    for hh in range(N_HEADS):
        vh = vt[hh * V_DIM:(hh + 1) * V_DIM]
        v4_ref[pl.ds(hh, tm, stride=N_HEADS), :] = vh.T
        vte_ref[hh * VE_ROWS:hh * VE_ROWS + V_DIM, :] = vh.astype(BF16)
        vte_ref[hh * VE_ROWS + V_DIM:(hh + 1) * VE_ROWS, :] = ones
    xr_ref[...] = proj(3 * ATT_W, 3 * ATT_W + RG_W)
    gr_ref[...] = proj(3 * ATT_W + RG_W, 3 * ATT_W + 2 * RG_W)


def _inproj(x2, batch, tb, g1, w_in_bf, wqt, wkt, wvt, gq_col, gk_col):
    T, D = x2.shape
    seq = T // batch
    tm = min(512, seq)
    tiles = seq // tm
    row = lambda w: pl.BlockSpec((tm, w), lambda i: (i, 0))
    col = lambda r: pl.BlockSpec((None, r, tm), lambda i: (i, 0, 0))
    out_shape = (
        jax.ShapeDtypeStruct((T // tm, N_MAPS * V_DIM, tm), BF16),
        jax.ShapeDtypeStruct((batch, ATT_W, seq), F32),
        jax.ShapeDtypeStruct((T, N_MAPS * V_DIM), BF16),
        jax.ShapeDtypeStruct((T * N_HEADS, V_DIM), F32),
        jax.ShapeDtypeStruct((T // tm, N_HEADS * VE_ROWS, tm), BF16),
        jax.ShapeDtypeStruct((T, RG_W), F32), jax.ShapeDtypeStruct((T, RG_W), F32),
    )
    args = (x2, g1, w_in_bf, wqt, wkt, wvt, gq_col, gk_col)
    return pl.pallas_call(
        functools.partial(_inproj_kernel, tb=tb),
        grid=(T // tm,),
        in_specs=[row(D)] + [_full(a.shape) for a in args[1:]],
        out_specs=(col(N_MAPS * V_DIM),
                   pl.BlockSpec((None, ATT_W, tm), lambda i: (i // tiles, 0, i % tiles)),
                   row(N_MAPS * V_DIM),
                   pl.BlockSpec((tm * N_HEADS, V_DIM), lambda i: (i, 0)),
                   col(N_HEADS * VE_ROWS), row(RG_W), row(RG_W)),
        out_shape=out_shape,
        compiler_params=_cparams(("parallel",)),
        name="inproj",
    )(*args)


def _lam_value(lq1, lk1, lq2, lk2, lam_init):
    s1 = jnp.sum(lq1 * lk1, axis=-1, keepdims=True)
    s2 = jnp.sum(lq2 * lk2, axis=-1, keepdims=True)
    return jnp.exp(s1) - jnp.exp(s2) + lam_init


def _subln(o, g, lam_init):
    ms = jnp.mean(o * o, axis=-1, keepdims=True)
    return (o * lax.rsqrt(ms + EPS) * g) * (1.0 - lam_init)


def _prompt_attn_kernel(it_ref, jt_ref, qt_ref, kbe_ref, vte_ref, lq1_ref, lk1_ref, lq2_ref, lk2_ref,
                        gsc_ref, o_ref, m_scr, acc_scr, mask_scr, *, tb, strip, lam_init):
    pidx = pl.program_id(1)
    i = it_ref[pidx]
    j = jt_ref[pidx]
    slopes2 = _alibi_slopes2()

    @pl.when(pidx == 0)
    def _():
        kl = lax.broadcasted_iota(jnp.int32, (tb, tb), 0)
        ql = lax.broadcasted_iota(jnp.int32, (tb, tb), 1)
        mask_scr[...] = jnp.where(kl <= ql, 0.0, NEG_INF)

    @pl.when(j == 0)
    def _():
        m_scr[...] = jnp.full(m_scr.shape, NEG_INF, F32)
        acc_scr[...] = jnp.zeros(acc_scr.shape, F32)

    gap = ((i - j) * tb).astype(F32)

    def update(diagonal):
        for mm in range(N_MAPS):
            hh = mm // 2
            off = -slopes2[hh] * gap
            for s0 in range(0, tb, strip):
                nk = s0 + strip if diagonal else tb
                lanes = slice(s0, s0 + strip)
                u = jnp.dot(kbe_ref[:nk, mm * V_DIM:(mm + 1) * V_DIM],
                            qt_ref[mm * V_DIM:(mm + 1) * V_DIM, lanes], preferred_element_type=F32)
                if diagonal:
                    u = u + mask_scr[:nk, lanes]
                m_prev = m_scr[mm, :, lanes]
                m_new = jnp.maximum(m_prev, jnp.max(u, axis=0, keepdims=True) + off)
                alpha = jnp.exp2(m_prev - m_new)
                p = jnp.exp2(u - (m_new - off)).astype(BF16)
                acc_scr[mm, :, lanes] = alpha * acc_scr[mm, :, lanes] + jnp.dot(
                    vte_ref[hh * VE_ROWS:(hh + 1) * VE_ROWS, :nk], p, preferred_element_type=F32)
                m_scr[mm, :, lanes] = m_new

    @pl.when(j < i)
    def _():
        update(False)

    @pl.when(j == i)
    def _():
        update(True)
        lam = _lam_value(lq1_ref[...], lk1_ref[...], lq2_ref[...], lk2_ref[...], lam_init)
        for hh in range(N_HEADS):
            a1 = acc_scr[2 * hh]
            a2 = acc_scr[2 * hh + 1]
            o1 = a1[:V_DIM] / a1[V_DIM:V_DIM + 1]
            o2 = a2[:V_DIM] / a2[V_DIM:V_DIM + 1]
            o = o1 - lam * o2
            ms = jnp.mean(o * o, axis=0, keepdims=True)
            y = (o * lax.rsqrt(ms + EPS) * gsc_ref[...]) * (1.0 - lam_init)
            o_ref[:, hh * V_DIM:(hh + 1) * V_DIM] = y.T.astype(o_ref.dtype)


def _prompt_attention(qt, kbe, vte, lq1, lk1, lq2, lk2, gs_col, batch, seq, tb, lam_init):
    nb = seq // tb
    strip = tb
    pairs = [(i, j) for i in range(nb) for j in range(i + 1)]
    i_tab = jnp.asarray([p[0] for p in pairs], jnp.int32)
    j_tab = jnp.asarray([p[1] for p in pairs], jnp.int32)
    cst = lambda a: pl.BlockSpec(a.shape, lambda b, p, it, jt: (0,) * a.ndim)
    kern = functools.partial(_prompt_attn_kernel, tb=tb, strip=strip, lam_init=lam_init)
    grid_spec = pltpu.PrefetchScalarGridSpec(
        num_scalar_prefetch=2,
        grid=(batch, len(pairs)),
        in_specs=[pl.BlockSpec((None, N_MAPS * V_DIM, tb), lambda b, p, it, jt: (b * nb + it[p], 0, 0)),
                  pl.BlockSpec((tb, N_MAPS * V_DIM), lambda b, p, it, jt: (b * nb + jt[p], 0)),
                  pl.BlockSpec((None, N_HEADS * VE_ROWS, tb), lambda b, p, it, jt: (b * nb + jt[p], 0, 0)),
                  cst(lq1), cst(lk1), cst(lq2), cst(lk2), cst(gs_col)],
        out_specs=pl.BlockSpec((tb, ATT_W), lambda b, p, it, jt: (b * nb + it[p], 0)),
        scratch_shapes=[pltpu.VMEM((N_MAPS, 1, tb), F32),
                        pltpu.VMEM((N_MAPS, VE_ROWS, tb), F32),
                        pltpu.VMEM((tb, tb), F32)],
    )
    return pl.pallas_call(
        kern,
        grid_spec=grid_spec,
        out_shape=jax.ShapeDtypeStruct((batch * seq, ATT_W), BF16),
        compiler_params=_cparams(("arbitrary", "arbitrary")),
        name="prompt_attn",
    )(i_tab, j_tab, qt, kbe, vte, lq1, lk1, lq2, lk2, gs_col)


def _decode_attn_kernel(pt_ref, qb_ref, kn_ref, vn_ref, sl_ref, lq1_ref, lk1_ref, lq2_ref, lk2_ref,
                        gs_ref, ck_hbm, cv_hbm, o_ref, kbuf, vbuf, sems, m_scr, l_scr, acc_scr,
                        *, pps, ng, past, lam_init):
    s = pl.program_id(0)
    n_steps = pl.num_programs(0)
    g = s % ng
    slot = s % 2

    def page_copies(step, sl):
        cps = []
        for pi in range(pps):
            page = pt_ref[step * pps + pi]
            cps.append(pltpu.make_async_copy(ck_hbm.at[page], kbuf.at[sl, pi], sems.at[0, sl]))
            cps.append(pltpu.make_async_copy(cv_hbm.at[page], vbuf.at[sl, pi], sems.at[1, sl]))
        return cps

    @pl.when(s == 0)
    def _():
        for cp in page_copies(0, 0):
            cp.start()

    @pl.when(s + 1 < n_steps)
    def _():
        for cp in page_copies(s + 1, 1 - slot):
            cp.start()

    for cp in page_copies(s, slot):
        cp.wait()
    kp_refs = [kbuf.at[slot, pi] for pi in range(pps)]
    vp_refs = [vbuf.at[slot, pi] for pi in range(pps)]

    @pl.when(g == 0)
    def _():
        m_scr[...] = jnp.full(m_scr.shape, NEG_INF, F32)
        l_scr[...] = jnp.zeros(l_scr.shape, F32)
        acc_scr[...] = jnp.zeros(acc_scr.shape, F32)

    q = qb_ref[...]
    slopes = sl_ref[...]
    lane = lax.broadcasted_iota(jnp.int32, (1, PAGE), 1)
    head_of_row = lax.broadcasted_iota(jnp.int32, (N_MAPS, 1), 0) // 2
    m_run = m_scr[...]
    l_run = l_scr[...]
    acc = acc_scr[...]
    scores = []
    for pi in range(pps):
        kpos = (g * pps + pi) * PAGE + lane
        dist = (past - kpos).astype(F32)
        sc = jnp.dot(q, kp_refs[pi][...].astype(BF16), preferred_element_type=F32)
        scores.append(sc - slopes * dist)
    smax = scores[0]
    for sc in scores[1:]:
        smax = jnp.maximum(smax, sc)
    m_new = jnp.maximum(m_run, jnp.max(smax, axis=-1, keepdims=True))
    alpha = jnp.exp2(m_run - m_new)
    psum = jnp.zeros((N_MAPS, PAGE), F32)
    pvs = [jnp.zeros((N_MAPS, V_DIM), F32) for _ in range(N_HEADS)]
    for pi in range(pps):
        p = jnp.exp2(scores[pi] - m_new)
        psum = psum + p
        pb = p.astype(BF16)
        for hh in range(N_HEADS):
            vh = vp_refs[pi][pl.ds(hh, PAGE, stride=N_HEADS), :].astype(BF16)
            pvs[hh] = pvs[hh] + jnp.dot(pb, vh, preferred_element_type=F32)
    pv = pvs[0]
    for hh in range(1, N_HEADS):
        pv = jnp.where(head_of_row == hh, pvs[hh], pv)
    m_scr[...] = m_new
    l_run = alpha * l_run + jnp.sum(psum, axis=-1, keepdims=True)
    acc = alpha * acc + pv
    l_scr[...] = l_run
    acc_scr[...] = acc
    m_run = m_new

    @pl.when(g == ng - 1)
    def _():
        kn = kn_ref[...].astype(BF16).astype(F32)
        vn = vn_ref[...].astype(BF16).astype(F32)
        s = jnp.sum(q.astype(F32) * kn, axis=-1, keepdims=True)
        m_new = jnp.maximum(m_run, s)
        alpha = jnp.exp2(m_run - m_new)
        p = jnp.exp2(s - m_new)
        l_fin = alpha * l_run + p
        acc_fin = alpha * acc + p.astype(BF16).astype(F32) * vn
        lam = _lam_value(lq1_ref[...], lk1_ref[...], lq2_ref[...], lk2_ref[...], lam_init)
        outn = acc_fin / l_fin
        for hh in range(N_HEADS):
            o = outn[2 * hh:2 * hh + 1] - lam * outn[2 * hh + 1:2 * hh + 2]
            o_ref[:, hh * V_DIM:(hh + 1) * V_DIM] = _subln(o, gs_ref[...], lam_init)


def _decode_attention(qblk, k_new, v_rows, slopes, lq1, lk1, lq2, lk2, gs, cache_kt, cache_v2,
                      page_table, lam_init):
    nb, n_pages = page_table.shape
    pps = 32
    while n_pages % pps:
        pps //= 2
    ng = n_pages // pps
    past = n_pages * PAGE
    pt_flat = page_table.reshape(-1)
    cst = lambda shape: pl.BlockSpec(shape, lambda s, pt: (0,) * len(shape))
    per_seq = lambda r, w: pl.BlockSpec((None, r, w), lambda s, pt: (s // ng, 0, 0))
    kern = functools.partial(_decode_attn_kernel, pps=pps, ng=ng, past=past, lam_init=lam_init)
    grid_spec = pltpu.PrefetchScalarGridSpec(
        num_scalar_prefetch=1,
        grid=(nb * ng,),
        in_specs=[per_seq(N_MAPS, ATT_W), per_seq(1, ATT_W), per_seq(N_MAPS, V_DIM),
                  cst(slopes.shape), cst(lq1.shape), cst(lk1.shape), cst(lq2.shape), cst(lk2.shape),
                  cst(gs.shape),
                  pl.BlockSpec(memory_space=pl.ANY), pl.BlockSpec(memory_space=pl.ANY)],
        out_specs=per_seq(1, ATT_W),
        scratch_shapes=[pltpu.VMEM((2, pps, ATT_W, PAGE), F32),
                        pltpu.VMEM((2, pps, PAGE * N_HEADS, V_DIM), F32),
                        pltpu.SemaphoreType.DMA((2, 2)),
                        pltpu.VMEM((N_MAPS, 1), F32), pltpu.VMEM((N_MAPS, 1), F32),
                        pltpu.VMEM((N_MAPS, V_DIM), F32)],
    )
    return pl.pallas_call(
        kern,
        grid_spec=grid_spec,
        out_shape=jax.ShapeDtypeStruct((nb, 1, ATT_W), F32),
        compiler_params=_cparams(("arbitrary",)),
        name="decode_attn",
    )(pt_flat, qblk, k_new, v_rows, slopes, lq1, lk1, lq2, lk2, gs, cache_kt, cache_v2)


def _expm1(y):
    acc = 1.0 + y * (1.0 / 10.0)
    for n in range(9, 1, -1):
        acc = 1.0 + (y * (1.0 / n)) * acc
    return jnp.where(jnp.abs(y) < 0.125, y * acc, jnp.exp(y) - 1.0)


def _log1p(z):
    return jnp.where(z < 1e-4, z * (1.0 - z * (0.5 - z * (1.0 / 3.0))), jnp.log(1.0 + z))


def _rg_gates(xc, wa_ref, ba_ref, wx_ref, bx_ref, lam_ref):
    xcb = xc.astype(BF16)
    r = _sigmoid(jnp.dot(xcb, wa_ref[...], preferred_element_type=F32) + ba_ref[...])
    ig = _sigmoid(jnp.dot(xcb, wx_ref[...], preferred_element_type=F32) + bx_ref[...])
    nl = -lam_ref[...]
    sp = jnp.maximum(nl, 0.0) + _log1p(jnp.exp(-jnp.abs(nl)))
    log_a = -RG_C * r * sp
    a = jnp.exp(log_a)
    bt = jnp.sqrt(-_expm1(2.0 * log_a)) * (ig * xc)
    return a, bt


def _rglru_seq_kernel(xr_ref, gr_ref, cw_ref, cb_ref, wa_ref, ba_ref, wx_ref, bx_ref, lam_ref,
                      rg_ref, ctail_ref, htail_ref, prev_scr, h_scr, *, ts):
    s = pl.program_id(1)

    @pl.when(s == 0)
    def _():
        prev_scr[...] = jnp.zeros(prev_scr.shape, F32)
        h_scr[...] = jnp.zeros(h_scr.shape, F32)

    x = xr_ref[...]
    prev = prev_scr[...]
    row8 = lax.broadcasted_iota(jnp.int32, (SUBLANES, 1), 0)
    cw = cw_ref[...]
    xc = cb_ref[...] + x * cw[CONV_W - 1:CONV_W]
    for jj in range(1, CONV_W):
        xs = pltpu.roll(x, jj, axis=0)
        ps = pltpu.roll(prev, jj, axis=0)
        top = jnp.where(row8 < jj, ps, xs[:SUBLANES])
        xs = jnp.concatenate([top, xs[SUBLANES:]], axis=0)
        xc = xc + xs * cw[CONV_W - 1 - jj:CONV_W - jj]
    a, bt = _rg_gates(xc, wa_ref, ba_ref, wx_ref, bx_ref, lam_ref)
    grp = min(32, ts)
    row_in_grp = lax.broadcasted_iota(jnp.int32, (ts, 1), 0) % grp
    d = 1
    while d < grp:
        a_s = pltpu.roll(a, d, axis=0)
        b_s = pltpu.roll(bt, d, axis=0)
        keep = row_in_grp >= d
        bt = jnp.where(keep, a * b_s + bt, bt)
        a = jnp.where(keep, a * a_s, a)
        d *= 2
    carry = h_scr[...]
    parts = []
    for g0 in range(0, ts, grp):
        hg = a[g0:g0 + grp] * carry + bt[g0:g0 + grp]
        parts.append(hg)
        carry = hg[grp - 1:grp]
    h = jnp.concatenate(parts, axis=0)
    rg_ref[...] = (h * jax.nn.gelu(gr_ref[...])).astype(rg_ref.dtype)
    h_scr[...] = h[ts - 1:ts]
    prev_scr[...] = x[ts - SUBLANES:ts]
    ctail_ref[...] = x[ts - SUBLANES:ts]
    htail_ref[...] = h[ts - SUBLANES:ts]


def _rglru_seq(xr, gr, cw, cb, wa_bd, ba, wx_bd, bx, lam, batch, seq):
    ts = min(256, seq)
    ns = seq // ts
    kern = functools.partial(_rglru_seq_kernel, ts=ts)
    cst = lambda a: pl.BlockSpec(a.shape, lambda b, s: (0,) * a.ndim)
    tile = pl.BlockSpec((ts, RG_W), lambda b, s: (b * ns + s, 0))
    tail = pl.BlockSpec((None, SUBLANES, RG_W), lambda b, s: (b, 0, 0))
    return pl.pallas_call(
        kern,
        grid=(batch, ns),
        in_specs=[tile, tile, cst(cw), cst(cb), cst(wa_bd), cst(ba), cst(wx_bd), cst(bx), cst(lam)],
        out_specs=(tile, tail, tail),
        out_shape=(jax.ShapeDtypeStruct((batch * seq, RG_W), BF16),
                   jax.ShapeDtypeStruct((batch, SUBLANES, RG_W), F32),
                   jax.ShapeDtypeStruct((batch, SUBLANES, RG_W), F32)),
        scratch_shapes=[pltpu.VMEM((SUBLANES, RG_W), F32), pltpu.VMEM((1, RG_W), F32)],
        compiler_params=_cparams(("parallel", "arbitrary")),
        name="rglru_seq",
    )(xr, gr, cw, cb, wa_bd, ba, wx_bd, bx, lam)


def _rglru_step_kernel(xr_ref, gr_ref, c0_ref, c1_ref, c2_ref, h0_ref, cw_ref, cb_ref, wa_ref, ba_ref,
                       wx_ref, bx_ref, lam_ref, rg_ref, h_ref):
    x = xr_ref[...]
    cw = cw_ref[...]
    xc = cb_ref[...] + c0_ref[...] * cw[0:1]
    xc = xc + c1_ref[...] * cw[1:2]
    xc = xc + c2_ref[...] * cw[2:3]
    xc = xc + x * cw[3:4]
    a, bt = _rg_gates(xc, wa_ref, ba_ref, wx_ref, bx_ref, lam_ref)
    h = a * h0_ref[...] + bt
    h_ref[...] = h
    rg_ref[...] = (h * jax.nn.gelu(gr_ref[...])).astype(rg_ref.dtype)


def _rglru_step(xr, gr, conv_state, h0, cw, cb, wa_bd, ba, wx_bd, bx, lam):
    nb = xr.shape[0]
    args = (xr, gr, conv_state[:, 0], conv_state[:, 1], conv_state[:, 2], h0,
            cw, cb, wa_bd, ba, wx_bd, bx, lam)
    return pl.pallas_call(
        _rglru_step_kernel,
        grid=(1,),
        in_specs=[_full(a.shape) for a in args],
        out_specs=(_full((nb, RG_W)), _full((nb, RG_W))),
        out_shape=(jax.ShapeDtypeStruct((nb, RG_W), BF16), jax.ShapeDtypeStruct((nb, RG_W), F32)),
        compiler_params=_cparams(("arbitrary",)),
        name="rglru_step",
    )(*args)


def _outproj_router_kernel(att_ref, rg_ref, x_ref, wo_ref, g2_ref, wr_ref, rb_ref, tri_ref,
                           x1_ref, hm8_ref, hmb_ref, idx_ref, gate_ref, rank_ref, cnt_ref,
                           carry_scr, *, tm, tiles_per_super):
    i = pl.program_id(0)

    @pl.when(i % tiles_per_super == 0)
    def _():
        carry_scr[...] = jnp.zeros(carry_scr.shape, F32)

    mixed = (jnp.dot(att_ref[...], wo_ref[:ATT_W, :], preferred_element_type=F32)
             + jnp.dot(rg_ref[...], wo_ref[ATT_W:, :], preferred_element_type=F32))
    x1 = x_ref[...] + mixed
    x1_ref[...] = x1
    ms = jnp.mean(x1 * x1, axis=-1, keepdims=True)
    hm = x1 * lax.rsqrt(ms + EPS) * g2_ref[...]
    hmb = hm.astype(BF16)
    hmb_ref[...] = hmb
    for s in range(SUBLANES):
        hm8_ref[pl.ds(s, tm, stride=SUBLANES), :] = hm[:, s * LANES:(s + 1) * LANES]

    scores = _sigmoid(_nt_dot(wr_ref[...], hmb))
    choice = scores + rb_ref[...]
    ch3 = choice.reshape(N_GROUPS, GROUP_SIZE, tm)
    i8 = lax.broadcasted_iota(jnp.int32, (N_GROUPS, GROUP_SIZE, tm), 1)
    m1 = jnp.max(ch3, axis=1, keepdims=True)
    f1 = jnp.min(jnp.where(ch3 == m1, i8, GROUP_SIZE), axis=1, keepdims=True)
    m2 = jnp.max(jnp.where(i8 == f1, -jnp.inf, ch3), axis=1, keepdims=True)
    gs3 = m1 + m2
    gs = gs3.reshape(N_GROUPS, tm)
    gi = lax.broadcasted_iota(jnp.int32, (N_GROUPS, tm), 0)
    grank = jnp.zeros((N_GROUPS, tm), jnp.int32)
    for g2 in range(N_GROUPS):
        rowv = gs[g2:g2 + 1, :]
        beats = (rowv > gs) | ((rowv == gs) & (g2 < gi))
        grank = grank + beats.astype(jnp.int32)
    gsel = (grank < TOPK_GROUPS).reshape(N_GROUPS, 1, tm)
    masked = jnp.where(gsel, ch3, NEG_INF).reshape(N_EXPERTS, tm)

    ei = lax.broadcasted_iota(jnp.int32, (N_EXPERTS, tm), 0)
    picks, gvals = [], []
    sel = jnp.zeros((N_EXPERTS, tm), F32)
    for _ in range(TOP_K):
        mx = jnp.max(masked, axis=0, keepdims=True)
        first = jnp.min(jnp.where(masked == mx, ei, N_EXPERTS), axis=0, keepdims=True)
        hit = ei == first
        picks.append(first)
        gvals.append(jnp.sum(jnp.where(hit, scores, 0.0), axis=0, keepdims=True))
        sel = jnp.where(hit, 1.0, sel)
        masked = jnp.where(hit, -jnp.inf, masked)
    gsum = gvals[0]
    for kk in range(1, TOP_K):
        gsum = gsum + gvals[kk]

    pos = jnp.dot(sel.astype(BF16), tri_ref[...], preferred_element_type=F32) + carry_scr[...]
    for kk in range(TOP_K):
        idx_ref[kk:kk + 1, :] = picks[kk]
        gate_ref[kk:kk + 1, :] = gvals[kk] / gsum * ROUTE_SCALE
        rk = jnp.sum(jnp.where(ei == picks[kk], pos, 0.0), axis=0, keepdims=True)
        rank_ref[kk:kk + 1, :] = rk.astype(jnp.int32)
    carry = carry_scr[...] + jnp.sum(sel, axis=1, keepdims=True)
    carry_scr[...] = carry
    cnt_ref[...] = jnp.broadcast_to(carry, cnt_ref.shape)


def _outproj_router(att, rg, x2, wo_bf, g2, wr_t, rb_col, tt):
    T, D = x2.shape
    tm = min(512, T)
    tiles_per_super = tt // tm
    n_super = T // tt
    tri = jnp.triu(jnp.ones((tm, tm), BF16), k=1)
    kern = functools.partial(_outproj_router_kernel, tm=tm, tiles_per_super=tiles_per_super)
    row = lambda w: pl.BlockSpec((tm, w), lambda i: (i, 0))
    col = pl.BlockSpec((TOP_K, tm), lambda i: (0, i))
    return pl.pallas_call(
        kern,
        grid=(T // tm,),
        in_specs=[row(ATT_W), row(RG_W), row(D), _full(wo_bf.shape), _full(g2.shape),
                  _full(wr_t.shape), _full(rb_col.shape), _full(tri.shape)],
        out_specs=(row(D), pl.BlockSpec((tm * SUBLANES, LANES), lambda i: (i, 0)), row(D),
                   col, col, col,
                   pl.BlockSpec((None, N_EXPERTS, LANES), lambda i: (i // tiles_per_super, 0, 0))),
        out_shape=(jax.ShapeDtypeStruct((T, D), F32),
                   jax.ShapeDtypeStruct((T * SUBLANES, LANES), F32),
                   jax.ShapeDtypeStruct((T, D), BF16),
                   jax.ShapeDtypeStruct((TOP_K, T), jnp.int32),
                   jax.ShapeDtypeStruct((TOP_K, T), F32),
                   jax.ShapeDtypeStruct((TOP_K, T), jnp.int32),
                   jax.ShapeDtypeStruct((n_super, N_EXPERTS, LANES), F32)),
        scratch_shapes=[pltpu.VMEM((N_EXPERTS, 1), F32)],
        compiler_params=_cparams(("arbitrary",)),
        name="outproj_router",
    )(att, rg, x2, wo_bf, g2, wr_t, rb_col, tri)


def _moe_kernel(cnt_ref, off_ref, hm8_ref, w13_ref, w2_ref, idx_ref, rank_ref, gate_ref, y8_ref,
                dest_v, dest_s, gate_s, row_s, gsort_s, xg, og, sem, *, tt, mch, d_model, d_exp):
    ti = pl.program_id(0)
    e = pl.program_id(1)
    chunks = d_model // LANES

    def rows_at(off):
        return pl.ds(pl.multiple_of(off, chunks), chunks)

    @pl.when(e == 0)
    def _():
        idx = idx_ref[...]
        dest = rank_ref[...]
        for ee in range(N_EXPERTS):
            dest = dest + jnp.where(idx == ee, off_ref[ti * N_EXPERTS + ee], 0)
        dest_v[...] = dest
        copies = []
        for kk in range(TOP_K):
            span = pl.ds(kk * tt, tt)
            copies.append(pltpu.make_async_copy(dest_v.at[kk], dest_s.at[span], sem.at[0]))
            copies.append(pltpu.make_async_copy(gate_ref.at[kk], gate_s.at[span], sem.at[1]))
        for cp in copies:
            cp.start()
        y8_ref[...] = jnp.zeros(y8_ref.shape, F32)
        xg[...] = jnp.zeros(xg.shape, F32)
        og[...] = jnp.zeros(og.shape, F32)
        for cp in copies:
            cp.wait()

        def invert(io, _):
            for u in range(SUBLANES):
                t = io * SUBLANES + u
                picks = [(dest_s[kk * tt + t], gate_s[kk * tt + t]) for kk in range(TOP_K)]
                for slot, gt in picks:
                    row_s[slot] = t * chunks
                    gsort_s[slot] = gt
            return 0

        lax.fori_loop(0, tt // SUBLANES, invert, 0)

    c = cnt_ref[ti * N_EXPERTS + e]
    off = off_ref[ti * N_EXPERTS + e]
    nch = (c + mch - 1) // mch
    g_unroll = 32
    s_unroll = 8

    def chunk(ch, carry):
        base = off + ch * mch
        nvalid = jnp.minimum(c - ch * mch, mch)

        def gather_row(ii):
            xg[rows_at(ii * chunks), :] = hm8_ref[rows_at(row_s[base + ii]), :]

        def gather_group(io, _):
            for u in range(g_unroll):
                gather_row(io * g_unroll + u)
            return 0

        def gather_one(ii, _):
            gather_row(ii)
            return 0

        n_full = nvalid // g_unroll
        lax.fori_loop(0, n_full, gather_group, 0)
        lax.fori_loop(n_full * g_unroll, nvalid, gather_one, 0)
        xb = jnp.concatenate([xg[pl.ds(s, mch, stride=chunks), :] for s in range(chunks)],
                             axis=-1).astype(BF16)
        h13 = jnp.dot(xb, w13_ref[...], preferred_element_type=F32)
        h1 = h13[:, :d_exp]
        hdn = (h1 * _sigmoid(h1)) * h13[:, d_exp:]
        o = jnp.dot(hdn.astype(BF16), w2_ref[...], preferred_element_type=F32)
        for s in range(chunks):
            og[pl.ds(s, mch, stride=chunks), :] = o[:, s * LANES:(s + 1) * LANES]

        def scatter_rows(first, n):
            vals = []
            for u in range(n):
                ii = first + u
                r = row_s[base + ii]
                gt = gsort_s[base + ii]
                vals.append((r, y8_ref[rows_at(r), :] + gt * og[rows_at(ii * chunks), :]))
            for r, v in vals:
                y8_ref[rows_at(r), :] = v

        def scatter_group(io, _):
            scatter_rows(io * s_unroll, s_unroll)
            return 0

        def scatter_one(ii, _):
            scatter_rows(ii, 1)
            return 0

        n_full_s = nvalid // s_unroll
        lax.fori_loop(0, n_full_s, scatter_group, 0)
        lax.fori_loop(n_full_s * s_unroll, nvalid, scatter_one, 0)
        return carry

    lax.fori_loop(0, nch, chunk, 0)


def _moe(hm8, w13, w2, idx, rank, gate, cnt_flat, off_flat, tt, mch):
    T8, _ = hm8.shape
    T = T8 // SUBLANES
    n_tiles = T // tt
    d_model = w13.shape[1]
    d_exp = w2.shape[1]
    assert d_model == SUBLANES * LANES and tt & (tt - 1) == 0 and mch % SUBLANES == 0
    kern = functools.partial(_moe_kernel, tt=tt, mch=mch, d_model=d_model, d_exp=d_exp)
    single = dict(pipeline_mode=pl.Buffered(1))
    picks = pl.BlockSpec((TOP_K, tt), lambda t, e, c, o: (0, t))
    grid_spec = pltpu.PrefetchScalarGridSpec(
        num_scalar_prefetch=2,
        grid=(n_tiles, N_EXPERTS),
        in_specs=[pl.BlockSpec((tt * SUBLANES, LANES), lambda t, e, c, o: (t, 0), **single),
                  pl.BlockSpec((None, d_model, 2 * d_exp), lambda t, e, c, o: (e, 0, 0)),
                  pl.BlockSpec((None, d_exp, d_model), lambda t, e, c, o: (e, 0, 0)),
                  picks, picks, picks],
        out_specs=pl.BlockSpec((tt * SUBLANES, LANES), lambda t, e, c, o: (t, 0), **single),
        scratch_shapes=[pltpu.VMEM((TOP_K, tt), jnp.int32),
                        pltpu.SMEM((tt * TOP_K,), jnp.int32), pltpu.SMEM((tt * TOP_K,), F32),
                        pltpu.SMEM((tt * TOP_K,), jnp.int32), pltpu.SMEM((tt * TOP_K,), F32),
                        pltpu.VMEM((mch * SUBLANES, LANES), F32),
                        pltpu.VMEM((mch * SUBLANES, LANES), F32),
                        pltpu.SemaphoreType.DMA((2,))],
    )
    return pl.pallas_call(
        kern,
        grid_spec=grid_spec,
        out_shape=jax.ShapeDtypeStruct((T8, LANES), F32),
        compiler_params=_cparams(("arbitrary", "arbitrary")),
        name="moe_experts",
    )(cnt_flat, off_flat, hm8, w13, w2, idx, rank, gate)


def _final_kernel(y8_ref, x1_ref, hmb_ref, ws13_ref, ws2_ref, gp_ref, wg_ref, p_ref, wp_ref, o_ref,
                  *, tm, d_sh):
    chunks = x1_ref.shape[1] // LANES
    routed = jnp.concatenate([y8_ref[pl.ds(s, tm, stride=chunks), :] for s in range(chunks)], axis=-1)
    hmb = hmb_ref[...]
    h13 = jnp.dot(hmb, ws13_ref[...], preferred_element_type=F32)
    h1 = h13[:, :d_sh]
    hdn = (h1 * _sigmoid(h1)) * h13[:, d_sh:]
    shared = jnp.dot(hdn.astype(BF16), ws2_ref[...], preferred_element_type=F32)
    x2 = x1_ref[...] + (routed + shared)
    ms = jnp.mean(x2 * x2, axis=-1, keepdims=True)
    hp = (x2 * lax.rsqrt(ms + EPS) * gp_ref[...]).astype(BF16)
    gate = _sigmoid(jnp.dot(hp, wg_ref[...], preferred_element_type=F32))
    proj = jnp.dot(p_ref[...].astype(BF16), wp_ref[...], preferred_element_type=F32)
    o_ref[...] = x2 + gate * proj


def _final(y8, x1, hmb, ws13, ws2, gp, wg, p2, wp):
    T, D = x1.shape
    tm = min(512, T)
    d_sh = ws2.shape[0]
    kern = functools.partial(_final_kernel, tm=tm, d_sh=d_sh)
    row = lambda w: pl.BlockSpec((tm, w), lambda i: (i, 0))
    return pl.pallas_call(
        kern,
        grid=(T // tm,),
        in_specs=[pl.BlockSpec((tm * SUBLANES, LANES), lambda i: (i, 0)), row(D), row(D),
                  _full(ws13.shape), _full(ws2.shape), _full(gp.shape), _full(wg.shape),
                  row(p2.shape[1]), _full(wp.shape)],
        out_specs=row(D),
        out_shape=jax.ShapeDtypeStruct((T, D), F32),
        compiler_params=_cparams(("parallel",)),
        name="final",
    )(y8, x1, hmb, ws13, ws2, gp, wg, p2, wp)


def _block_diag(w):
    nh, d, _ = w.shape
    eye = jnp.eye(nh, dtype=w.dtype)
    return (eye[:, None, :, None] * w[:, :, None, :]).reshape(nh * d, nh * d)


def _prep_weights(i, g_norm1, w_in, g_q, g_k, lam_q1, lam_k1, lam_q2, lam_k2, g_subln, conv_w, conv_b,
                  w_a, b_a, w_x, b_x, lam_rg, w_out, g_norm2, w_router, router_bias, w_e1, w_e3, w_e2,
                  w_s1, w_s3, w_s2, g_ple, w_ple_gate, w_ple_proj):
    return dict(
        g1=g_norm1[i][None], w_in=w_in[i].astype(BF16),
        wqt=w_in[i][:, :ATT_W].T.astype(BF16), wkt=w_in[i][:, ATT_W:2 * ATT_W].T.astype(BF16),
        wvt=w_in[i][:, 2 * ATT_W:3 * ATT_W].T.astype(BF16),
        gq_col=g_q[i][:, None], gk_col=g_k[i][:, None],
        lq1=lam_q1[i][None], lk1=lam_k1[i][None], lq2=lam_q2[i][None], lk2=lam_k2[i][None],
        gs=g_subln[i][None], gs_col=g_subln[i][:, None],
        cw=conv_w[i], cb=conv_b[i][None],
        wa=_block_diag(w_a[i]).astype(BF16), ba=b_a[i][None],
        wx=_block_diag(w_x[i]).astype(BF16), bx=b_x[i][None], lam_rg=lam_rg[i][None],
        wo=w_out[i].astype(BF16), g2=g_norm2[i][None],
        wr_t=w_router[i].T.astype(BF16), rb=router_bias[i][:, None],
        w13=jnp.concatenate([w_e1[i], w_e3[i]], axis=-1).astype(BF16), w2=w_e2[i].astype(BF16),
        ws13=jnp.concatenate([w_s1[i], w_s3[i]], axis=-1).astype(BF16), ws2=w_s2[i].astype(BF16),
        gp=g_ple[i][None], wg=w_ple_gate[i].astype(BF16), wp=w_ple_proj[i].astype(BF16),
    )


def _moe_stage(att, rg, x2, p2, W, tt, mch):
    T = x2.shape[0]
    x1, hm8, hmb, idx, gate, rank, cnt = _outproj_router(att, rg, x2, W['wo'], W['g2'], W['wr_t'],
                                                         W['rb'], tt)
    cnt_i = cnt[:, :, 0].astype(jnp.int32)
    off_i = jnp.cumsum(cnt_i, axis=1) - cnt_i
    y8 = _moe(hm8, W['w13'], W['w2'], idx, rank, gate, cnt_i.reshape(-1), off_i.reshape(-1), tt, mch)
    return _final(y8, x1, hmb, W['ws13'], W['ws2'], W['gp'], W['wg'], p2, W['wp'])


def _slopes_col():
    h = jnp.arange(N_MAPS) // 2 + 1
    return (LOG2E * jnp.exp2(-8.0 * h.astype(F32) / N_HEADS))[:, None]


def kernel(x_prompt, x_sample, cache_k, cache_v, state_conv, state_h, page_table, p_prompt, p_sample, g_norm1, w_in, g_q, g_k, lam_q1, lam_k1, lam_q2, lam_k2, g_subln, conv_w, conv_b, w_a, b_a, w_x, b_x, lam_rg, w_out, g_norm2, w_router, router_bias, w_e1, w_e3, w_e2, w_s1, w_s3, w_s2, g_ple, w_ple_gate, w_ple_proj):
    depth = w_in.shape[0]
    B, S, D = x_prompt.shape
    NB, NS, _ = x_sample.shape
    assert NS == 1
    n_pool = cache_k.shape[1]
    yp = x_prompt.reshape(B * S, D)
    ys = x_sample.reshape(NB * NS, D)
    kp_l, vp_l, cp_l, hp_l, ks_l, vs_l, cs_l, hs_l = [], [], [], [], [], [], [], []
    for i in range(depth):
        W = _prep_weights(i, g_norm1, w_in, g_q, g_k, lam_q1, lam_k1, lam_q2, lam_k2, g_subln, conv_w,
                          conv_b, w_a, b_a, w_x, b_x, lam_rg, w_out, g_norm2, w_router, router_bias,
                          w_e1, w_e3, w_e2, w_s1, w_s3, w_s2, g_ple, w_ple_gate, w_ple_proj)
        lam_init = 0.8 - 0.6 * math.exp(-0.3 * i)
        lam_vecs = (W['lq1'], W['lk1'], W['lq2'], W['lk2'])
        inproj_w = (W['g1'], W['w_in'], W['wqt'], W['wkt'], W['wvt'], W['gq_col'], W['gk_col'])

        tb = min(512, S)
        qt, kt, kbe, v4, vte, xr, gr = _inproj(yp, B, tb, *inproj_w)
        att = _prompt_attention(qt, kbe, vte, *lam_vecs, W['gs_col'], B, S, tb, lam_init)
        rg, ctail, htail = _rglru_seq(xr, gr, W['cw'], W['cb'], W['wa'], W['ba'], W['wx'], W['bx'],
                                      W['lam_rg'], B, S)
        tt_p = min(4096, B * S)
        mch_p = tt_p // SUBLANES + tt_p // 64
        yp = _moe_stage(att, rg, yp, p_prompt[i].reshape(B * S, -1), W, tt_p, mch_p)
        kp_l.append(jnp.transpose(kt.reshape(B, N_HEADS, 2, HEAD_DIM, S), (0, 4, 1, 2, 3)))
        vp_l.append(v4.reshape(B, S, N_HEADS, V_DIM))
        cp_l.append(ctail[:, SUBLANES - (CONV_W - 1):])
        hp_l.append(htail[:, SUBLANES - 1])

        qt, kt, _, v4, _, xr, gr = _inproj(ys, 1, PAGE, *inproj_w)
        k = kt[0].T
        v = v4.reshape(NB, N_HEADS, V_DIM)
        qm = qt[0].T.reshape(NB, N_MAPS, V_DIM)[:, :, :HEAD_DIM]
        msel = jnp.eye(N_MAPS, dtype=BF16)
        qblk = (qm[:, :, None, :] * msel[None, :, :, None]).reshape(NB, N_MAPS, ATT_W)
        v_rows = jnp.repeat(v, 2, axis=1)
        cache_kt = jnp.transpose(cache_k[i], (0, 2, 3, 4, 1)).reshape(n_pool, ATT_W, PAGE)
        cache_v2 = cache_v[i].reshape(n_pool, PAGE * N_HEADS, V_DIM)
        att_s = _decode_attention(qblk, k.reshape(NB, 1, ATT_W), v_rows, _slopes_col(), *lam_vecs,
                                  W['gs'], cache_kt, cache_v2, page_table, lam_init)
        rg_s, h_s = _rglru_step(xr, gr, state_conv[i], state_h[i], W['cw'], W['cb'], W['wa'], W['ba'],
                                W['wx'], W['bx'], W['lam_rg'])
        ys = _moe_stage(att_s.reshape(NB, ATT_W).astype(BF16), rg_s, ys,
                        p_sample[i].reshape(NB * NS, -1), W, NB, 32)
        ks_l.append(k.reshape(NB, NS, N_HEADS, 2, HEAD_DIM))
        vs_l.append(v.reshape(NB, NS, N_HEADS, V_DIM))
        cs_l.append(jnp.concatenate([state_conv[i][:, 1:], xr[:, None, :]], axis=1))
        hs_l.append(h_s)
    return (yp.reshape(B, S, D), ys.reshape(NB, NS, D), jnp.stack(kp_l), jnp.stack(vp_l),
            jnp.stack(cp_l), jnp.stack(hp_l), jnp.stack(ks_l), jnp.stack(vs_l), jnp.stack(cs_l),
            jnp.stack(hs_l))
```

```python
import functools
import math

import jax
import jax.numpy as jnp
from jax import lax
from jax.experimental import pallas as pl
from jax.experimental.pallas import tpu as pltpu

F32 = jnp.float32
BF16 = jnp.bfloat16

EPS = 1e-6
NEG_INF = -1e30
LANES = 128
SUBLANES = 8
N_HEADS = 4
HEAD_DIM = 64
V_DIM = 2 * HEAD_DIM
N_MAPS = 2 * N_HEADS
ATT_W = N_HEADS * V_DIM
RG_W = 512
RG_C = 8.0
CONV_W = 4
N_EXPERTS = 64
N_GROUPS = 8
GROUP_SIZE = N_EXPERTS // N_GROUPS
TOPK_GROUPS = 4
TOP_K = 8
ROUTE_SCALE = 2.5
ATT_SCALE = HEAD_DIM ** -0.5
LOG2E = math.log2(math.e)
VE_ROWS = V_DIM + 16
PAGE = 128
VMEM_LIMIT = 56 * 1024 * 1024


def _cparams(sem):
    return pltpu.CompilerParams(dimension_semantics=sem, vmem_limit_bytes=VMEM_LIMIT)


def _sigmoid(x):
    return 1.0 / (1.0 + jnp.exp(-x))


def _nt_dot(a, b):
    return lax.dot_general(a, b, (((1,), (1,)), ((), ())), preferred_element_type=F32)


def _full(shape):
    n = len(shape)
    return pl.BlockSpec(shape, lambda *_: (0,) * n)


def _alibi_slopes2():
    return [LOG2E * 2.0 ** (-8.0 * (hh + 1) / N_HEADS) for hh in range(N_HEADS)]


def _inproj_kernel(x_ref, g1_ref, w_ref, wqt_ref, wkt_ref, wvt_ref, gq_ref, gk_ref,
                   qt_ref, kt_ref, kbe_ref, v4_ref, vte_ref, xr_ref, gr_ref, *, tb):
    x = x_ref[...]
    tm = x.shape[0]
    ms = jnp.mean(x * x, axis=-1, keepdims=True)
    h = (x * lax.rsqrt(ms + EPS) * g1_ref[...]).astype(BF16)

    def proj(lo, hi):
        return jnp.dot(h, w_ref[:, lo:hi], preferred_element_type=F32)

    def feature_major_normed(wt_ref, g_ref):
        z = _nt_dot(wt_ref[...], h).reshape(N_MAPS, HEAD_DIM, tm)
        zms = jnp.mean(z * z, axis=1, keepdims=True)
        return z * lax.rsqrt(zms + EPS) * g_ref[...][None]

    qn = feature_major_normed(wqt_ref, gq_ref) * (ATT_SCALE * LOG2E)
    kn = feature_major_normed(wkt_ref, gk_ref)
    kt_ref[...] = kn.reshape(N_MAPS * HEAD_DIM, tm)

    row8 = lax.broadcasted_iota(jnp.int32, (SUBLANES, tm), 0)
    pad = jnp.zeros((V_DIM - HEAD_DIM - SUBLANES, tm), F32)
    q_tail = jnp.concatenate([jnp.where(row8 < 3, 1.0, 0.0), pad], axis=0).astype(BF16)
    tok = pl.program_id(0) * tm + lax.broadcasted_iota(jnp.int32, (1, tm), 1)
    k_local = (tok % tb).astype(F32)
    slopes2 = _alibi_slopes2()
    kte = []
    for mm in range(N_MAPS):
        qt_ref[mm * V_DIM:mm * V_DIM + HEAD_DIM, :] = qn[mm].astype(BF16)
        qt_ref[mm * V_DIM + HEAD_DIM:(mm + 1) * V_DIM, :] = q_tail
        val = slopes2[mm // 2] * k_local
        hi = val.astype(BF16).astype(F32)
        mid = (val - hi).astype(BF16).astype(F32)
        lo = ((val - hi) - mid).astype(BF16).astype(F32)
        k_tail = jnp.where(row8 == 0, hi, jnp.where(row8 == 1, mid, jnp.where(row8 == 2, lo, 0.0)))
        kte += [kn[mm], k_tail, pad]
    kbe_ref[...] = jnp.concatenate(kte, axis=0).T.astype(BF16)

    vt = _nt_dot(wvt_ref[...], h)
    ones = jnp.ones((VE_ROWS - V_DIM, tm), BF16)
    for hh in range(N_HEADS):
        vh = vt[hh * V_DIM:(hh + 1) * V_DIM]
        v4_ref[pl.ds(hh, tm, stride=N_HEADS), :] = vh.T
        vte_ref[hh * VE_ROWS:hh * VE_ROWS + V_DIM, :] = vh.astype(BF16)
        vte_ref[hh * VE_ROWS + V_DIM:(hh + 1) * VE_ROWS, :] = ones
    xr_ref[...] = proj(3 * ATT_W, 3 * ATT_W + RG_W)
    gr_ref[...] = proj(3 * ATT_W + RG_W, 3 * ATT_W + 2 * RG_W)


def _inproj(x2, batch, tb, g1, w_in_bf, wqt, wkt, wvt, gq_col, gk_col):
    T, D = x2.shape
    seq = T // batch
    tm = min(512, seq)
    tiles = seq // tm
    row = lambda w: pl.BlockSpec((tm, w), lambda i: (i, 0))
    col = lambda r: pl.BlockSpec((None, r, tm), lambda i: (i, 0, 0))
    out_shape = (
        jax.ShapeDtypeStruct((T // tm, N_MAPS * V_DIM, tm), BF16),
        jax.ShapeDtypeStruct((batch, ATT_W, seq), F32),
        jax.ShapeDtypeStruct((T, N_MAPS * V_DIM), BF16),
        jax.ShapeDtypeStruct((T * N_HEADS, V_DIM), F32),
        jax.ShapeDtypeStruct((T // tm, N_HEADS * VE_ROWS, tm), BF16),
        jax.ShapeDtypeStruct((T, RG_W), F32), jax.ShapeDtypeStruct((T, RG_W), F32),
    )
    args = (x2, g1, w_in_bf, wqt, wkt, wvt, gq_col, gk_col)
    return pl.pallas_call(
        functools.partial(_inproj_kernel, tb=tb),
        grid=(T // tm,),
        in_specs=[row(D)] + [_full(a.shape) for a in args[1:]],
        out_specs=(col(N_MAPS * V_DIM),
                   pl.BlockSpec((None, ATT_W, tm), lambda i: (i // tiles, 0, i % tiles)),
                   row(N_MAPS * V_DIM),
                   pl.BlockSpec((tm * N_HEADS, V_DIM), lambda i: (i, 0)),
                   col(N_HEADS * VE_ROWS), row(RG_W), row(RG_W)),
        out_shape=out_shape,
        compiler_params=_cparams(("parallel",)),
        name="inproj",
    )(*args)


def _lam_value(lq1, lk1, lq2, lk2, lam_init):
    s1 = jnp.sum(lq1 * lk1, axis=-1, keepdims=True)
    s2 = jnp.sum(lq2 * lk2, axis=-1, keepdims=True)
    return jnp.exp(s1) - jnp.exp(s2) + lam_init


def _subln(o, g, lam_init):
    ms = jnp.mean(o * o, axis=-1, keepdims=True)
    return (o * lax.rsqrt(ms + EPS) * g) * (1.0 - lam_init)


def _prompt_attn_step(pidx, it_ref, jt_ref, qt_ref, kbe_ref, vte_ref, lq1_ref, lk1_ref, lq2_ref,
                      lk2_ref, gsc_ref, o_ref, m_scr, acc_scr, mask_scr, *, tb, strip, lam_init):
    i = it_ref[pidx]
    j = jt_ref[pidx]
    slopes2 = _alibi_slopes2()

    @pl.when(pidx == 0)
    def _():
        kl = lax.broadcasted_iota(jnp.int32, (tb, tb), 0)
        ql = lax.broadcasted_iota(jnp.int32, (tb, tb), 1)
        mask_scr[...] = jnp.where(kl <= ql, 0.0, NEG_INF)

    @pl.when(j == 0)
    def _():
        m_scr[...] = jnp.full(m_scr.shape, NEG_INF, F32)
        acc_scr[...] = jnp.zeros(acc_scr.shape, F32)

    gap = ((i - j) * tb).astype(F32)

    def update(diagonal):
        for mm in range(N_MAPS):
            hh = mm // 2
            off = -slopes2[hh] * gap
            for s0 in range(0, tb, strip):
                nk = s0 + strip if diagonal else tb
                lanes = slice(s0, s0 + strip)
                u = jnp.dot(kbe_ref[:nk, mm * V_DIM:(mm + 1) * V_DIM],
                            qt_ref[mm * V_DIM:(mm + 1) * V_DIM, lanes], preferred_element_type=F32)
                if diagonal:
                    u = u + mask_scr[:nk, lanes]
                m_prev = m_scr[mm, :, lanes]
                m_new = jnp.maximum(m_prev, jnp.max(u, axis=0, keepdims=True) + off)
                alpha = jnp.exp2(m_prev - m_new)
                p = jnp.exp2(u - (m_new - off)).astype(BF16)
                acc_scr[mm, :, lanes] = alpha * acc_scr[mm, :, lanes] + jnp.dot(
                    vte_ref[hh * VE_ROWS:(hh + 1) * VE_ROWS, :nk], p, preferred_element_type=F32)
                m_scr[mm, :, lanes] = m_new

    @pl.when(j < i)
    def _():
        update(False)

    @pl.when(j == i)
    def _():
        update(True)
        lam = _lam_value(lq1_ref[...], lk1_ref[...], lq2_ref[...], lk2_ref[...], lam_init)
        for hh in range(N_HEADS):
            a1 = acc_scr[2 * hh]
            a2 = acc_scr[2 * hh + 1]
            o1 = a1[:V_DIM] / a1[V_DIM:V_DIM + 1]
            o2 = a2[:V_DIM] / a2[V_DIM:V_DIM + 1]
            o = o1 - lam * o2
            ms = jnp.mean(o * o, axis=0, keepdims=True)
            y = (o * lax.rsqrt(ms + EPS) * gsc_ref[...]) * (1.0 - lam_init)
            o_ref[:, hh * V_DIM:(hh + 1) * V_DIM] = y.T.astype(o_ref.dtype)


def _prompt_attn_kernel(*refs, **static):
    _prompt_attn_step(pl.program_id(1), *refs, **static)


def _attn_pairs(nb):
    pairs = [(i, j) for i in range(nb) for j in range(i + 1)]
    return (jnp.asarray([p[0] for p in pairs], jnp.int32),
            jnp.asarray([p[1] for p in pairs], jnp.int32), len(pairs))


def _prompt_attention(qt, kbe, vte, lq1, lk1, lq2, lk2, gs_col, batch, seq, tb, lam_init):
    nb = seq // tb
    strip = tb
    i_tab, j_tab, n_pairs = _attn_pairs(nb)
    cst = lambda a: pl.BlockSpec(a.shape, lambda b, p, it, jt: (0,) * a.ndim)
    kern = functools.partial(_prompt_attn_kernel, tb=tb, strip=strip, lam_init=lam_init)
    grid_spec = pltpu.PrefetchScalarGridSpec(
        num_scalar_prefetch=2,
        grid=(batch, n_pairs),
        in_specs=[pl.BlockSpec((None, N_MAPS * V_DIM, tb), lambda b, p, it, jt: (b * nb + it[p], 0, 0)),
                  pl.BlockSpec((tb, N_MAPS * V_DIM), lambda b, p, it, jt: (b * nb + jt[p], 0)),
                  pl.BlockSpec((None, N_HEADS * VE_ROWS, tb), lambda b, p, it, jt: (b * nb + jt[p], 0, 0)),
                  cst(lq1), cst(lk1), cst(lq2), cst(lk2), cst(gs_col)],
        out_specs=pl.BlockSpec((tb, ATT_W), lambda b, p, it, jt: (b * nb + it[p], 0)),
        scratch_shapes=[pltpu.VMEM((N_MAPS, 1, tb), F32),
                        pltpu.VMEM((N_MAPS, VE_ROWS, tb), F32),
                        pltpu.VMEM((tb, tb), F32)],
    )
    return pl.pallas_call(
        kern,
        grid_spec=grid_spec,
        out_shape=jax.ShapeDtypeStruct((batch * seq, ATT_W), BF16),
        compiler_params=_cparams(("arbitrary", "arbitrary")),
        name="prompt_attn",
    )(i_tab, j_tab, qt, kbe, vte, lq1, lk1, lq2, lk2, gs_col)


def _page_copies(step, sl, pt_ref, ck_hbm, cv_hbm, kbuf, vbuf, sems, pps):
    cps = []
    for pi in range(pps):
        page = pt_ref[step * pps + pi]
        cps.append(pltpu.make_async_copy(ck_hbm.at[page], kbuf.at[sl, pi], sems.at[0, sl]))
        cps.append(pltpu.make_async_copy(cv_hbm.at[page], vbuf.at[sl, pi], sems.at[1, sl]))
    return cps


def _decode_issue(s, n_steps, pt_ref, ck_hbm, cv_hbm, kbuf, vbuf, sems, pps):
    dma = (pt_ref, ck_hbm, cv_hbm, kbuf, vbuf, sems, pps)

    @pl.when(s == 0)
    def _():
        for cp in _page_copies(0, 0, *dma):
            cp.start()

    @pl.when(s + 1 < n_steps)
    def _():
        for cp in _page_copies(s + 1, 1 - s % 2, *dma):
            cp.start()


def _decode_compute(s, pt_ref, qb_ref, kn_ref, vn_ref, sl_ref, lq1_ref, lk1_ref, lq2_ref, lk2_ref,
                    gs_ref, ck_hbm, cv_hbm, o_ref, kbuf, vbuf, sems, m_scr, l_scr, acc_scr,
                    *, pps, ng, past, lam_init):
    g = s % ng
    slot = s % 2
    for cp in _page_copies(s, slot, pt_ref, ck_hbm, cv_hbm, kbuf, vbuf, sems, pps):
        cp.wait()
    kp_refs = [kbuf.at[slot, pi] for pi in range(pps)]
    vp_refs = [vbuf.at[slot, pi] for pi in range(pps)]

    @pl.when(g == 0)
    def _():
        m_scr[...] = jnp.full(m_scr.shape, NEG_INF, F32)
        l_scr[...] = jnp.zeros(l_scr.shape, F32)
        acc_scr[...] = jnp.zeros(acc_scr.shape, F32)

    q = qb_ref[...]
    slopes = sl_ref[...]
    lane = lax.broadcasted_iota(jnp.int32, (1, PAGE), 1)
    head_of_row = lax.broadcasted_iota(jnp.int32, (N_MAPS, 1), 0) // 2
    m_run = m_scr[...]
    l_run = l_scr[...]
    acc = acc_scr[...]
    scores = []
    for pi in range(pps):
        kpos = (g * pps + pi) * PAGE + lane
        dist = (past - kpos).astype(F32)
        sc = jnp.dot(q, kp_refs[pi][...].astype(BF16), preferred_element_type=F32)
        scores.append(sc - slopes * dist)
    smax = scores[0]
    for sc in scores[1:]:
        smax = jnp.maximum(smax, sc)
    m_new = jnp.maximum(m_run, jnp.max(smax, axis=-1, keepdims=True))
    alpha = jnp.exp2(m_run - m_new)
    psum = jnp.zeros((N_MAPS, PAGE), F32)
    pvs = [jnp.zeros((N_MAPS, V_DIM), F32) for _ in range(N_HEADS)]
    for pi in range(pps):
        p = jnp.exp2(scores[pi] - m_new)
        psum = psum + p
        pb = p.astype(BF16)
        for hh in range(N_HEADS):
            vh = vp_refs[pi][pl.ds(hh, PAGE, stride=N_HEADS), :].astype(BF16)
            pvs[hh] = pvs[hh] + jnp.dot(pb, vh, preferred_element_type=F32)
    pv = pvs[0]
    for hh in range(1, N_HEADS):
        pv = jnp.where(head_of_row == hh, pvs[hh], pv)
    m_scr[...] = m_new
    l_run = alpha * l_run + jnp.sum(psum, axis=-1, keepdims=True)
    acc = alpha * acc + pv
    l_scr[...] = l_run
    acc_scr[...] = acc
    m_run = m_new

    @pl.when(g == ng - 1)
    def _():
        kn = kn_ref[...].astype(BF16).astype(F32)
        vn = vn_ref[...].astype(BF16).astype(F32)
        s = jnp.sum(q.astype(F32) * kn, axis=-1, keepdims=True)
        m_new = jnp.maximum(m_run, s)
        alpha = jnp.exp2(m_run - m_new)
        p = jnp.exp2(s - m_new)
        l_fin = alpha * l_run + p
        acc_fin = alpha * acc + p.astype(BF16).astype(F32) * vn
        lam = _lam_value(lq1_ref[...], lk1_ref[...], lq2_ref[...], lk2_ref[...], lam_init)
        outn = acc_fin / l_fin
        for hh in range(N_HEADS):
            o = outn[2 * hh:2 * hh + 1] - lam * outn[2 * hh + 1:2 * hh + 2]
            o_ref[:, hh * V_DIM:(hh + 1) * V_DIM] = _subln(o, gs_ref[...], lam_init)


def _decode_attn_kernel(pt_ref, *refs, pps, **static):
    s = pl.program_id(0)
    ck_hbm, cv_hbm, _, kbuf, vbuf, sems = refs[9:15]
    _decode_issue(s, pl.num_programs(0), pt_ref, ck_hbm, cv_hbm, kbuf, vbuf, sems, pps)
    _decode_compute(s, pt_ref, *refs, pps=pps, **static)


def _decode_attention(qblk, k_new, v_rows, slopes, lq1, lk1, lq2, lk2, gs, cache_kt, cache_v2,
                      page_table, lam_init):
    nb, n_pages = page_table.shape
    pps = 32
    while n_pages % pps:
        pps //= 2
    ng = n_pages // pps
    past = n_pages * PAGE
    pt_flat = page_table.reshape(-1)
    cst = lambda shape: pl.BlockSpec(shape, lambda s, pt: (0,) * len(shape))
    per_seq = lambda r, w: pl.BlockSpec((None, r, w), lambda s, pt: (s // ng, 0, 0))
    kern = functools.partial(_decode_attn_kernel, pps=pps, ng=ng, past=past, lam_init=lam_init)
    grid_spec = pltpu.PrefetchScalarGridSpec(
        num_scalar_prefetch=1,
        grid=(nb * ng,),
        in_specs=[per_seq(N_MAPS, ATT_W), per_seq(1, ATT_W), per_seq(N_MAPS, V_DIM),
                  cst(slopes.shape), cst(lq1.shape), cst(lk1.shape), cst(lq2.shape), cst(lk2.shape),
                  cst(gs.shape),
                  pl.BlockSpec(memory_space=pl.ANY), pl.BlockSpec(memory_space=pl.ANY)],
        out_specs=per_seq(1, ATT_W),
        scratch_shapes=[pltpu.VMEM((2, pps, ATT_W, PAGE), F32),
                        pltpu.VMEM((2, pps, PAGE * N_HEADS, V_DIM), F32),
                        pltpu.SemaphoreType.DMA((2, 2)),
                        pltpu.VMEM((N_MAPS, 1), F32), pltpu.VMEM((N_MAPS, 1), F32),
                        pltpu.VMEM((N_MAPS, V_DIM), F32)],
    )
    return pl.pallas_call(
        kern,
        grid_spec=grid_spec,
        out_shape=jax.ShapeDtypeStruct((nb, 1, ATT_W), F32),
        compiler_params=_cparams(("arbitrary",)),
        name="decode_attn",
    )(pt_flat, qblk, k_new, v_rows, slopes, lq1, lk1, lq2, lk2, gs, cache_kt, cache_v2)


def _fused_attn_kernel(it_ref, jt_ref, pt_ref, qt_ref, kbe_ref, vte_ref, lq1_ref, lk1_ref, lq2_ref,
                       lk2_ref, gsc_ref, qb_ref, kn_ref, vn_ref, sl_ref, gs_ref, ck_hbm, cv_hbm,
                       o_ref, od_ref, m_scr, acc_scr, mask_scr, kbuf, vbuf, sems, dm_scr, dl_scr,
                       dacc_scr, *, tb, strip, pps, ng, n_units, past, lam_init):
    pidx = pl.program_id(1)
    s = pl.program_id(0) * pl.num_programs(1) + pidx

    @pl.when(s < n_units)
    def _():
        _decode_issue(s, n_units, pt_ref, ck_hbm, cv_hbm, kbuf, vbuf, sems, pps)

    _prompt_attn_step(pidx, it_ref, jt_ref, qt_ref, kbe_ref, vte_ref, lq1_ref, lk1_ref, lq2_ref,
                      lk2_ref, gsc_ref, o_ref, m_scr, acc_scr, mask_scr, tb=tb, strip=strip,
                      lam_init=lam_init)

    @pl.when(s < n_units)
    def _():
        _decode_compute(s, pt_ref, qb_ref, kn_ref, vn_ref, sl_ref, lq1_ref, lk1_ref, lq2_ref,
                        lk2_ref, gs_ref, ck_hbm, cv_hbm, od_ref, kbuf, vbuf, sems, dm_scr, dl_scr,
                        dacc_scr, pps=pps, ng=ng, past=past, lam_init=lam_init)


def _fused_decode_pages_per_step(n_steps, nb, n_pages):
    pps = 1
    while pps <= n_pages:
        if n_pages % pps == 0 and nb * (n_pages // pps) <= n_steps:
            return pps
        pps *= 2
    return None


def _fused_attention(qt, kbe, vte, lq1, lk1, lq2, lk2, gs_col, batch, seq, tb,
                     qblk, k_new, v_rows, slopes, gs, cache_kt, cache_v2, page_table, pps, lam_init):
    nblk = seq // tb
    i_tab, j_tab, n_pairs = _attn_pairs(nblk)
    nb, n_pages = page_table.shape
    ng = n_pages // pps
    n_units = nb * ng
    assert n_units <= batch * n_pairs
    past = n_pages * PAGE
    pt_flat = page_table.reshape(-1)
    cst = lambda a: pl.BlockSpec(a.shape, lambda b, p, it, jt, pt: (0,) * a.ndim)

    def per_seq(r, w):
        def index(b, p, it, jt, pt):
            return (jnp.minimum((b * n_pairs + p) // ng, nb - 1), 0, 0)
        return pl.BlockSpec((None, r, w), index)

    kern = functools.partial(_fused_attn_kernel, tb=tb, strip=tb, pps=pps, ng=ng, n_units=n_units,
                             past=past, lam_init=lam_init)
    grid_spec = pltpu.PrefetchScalarGridSpec(
        num_scalar_prefetch=3,
        grid=(batch, n_pairs),
        in_specs=[pl.BlockSpec((None, N_MAPS * V_DIM, tb),
                               lambda b, p, it, jt, pt: (b * nblk + it[p], 0, 0)),
                  pl.BlockSpec((tb, N_MAPS * V_DIM), lambda b, p, it, jt, pt: (b * nblk + jt[p], 0)),
                  pl.BlockSpec((None, N_HEADS * VE_ROWS, tb),
                               lambda b, p, it, jt, pt: (b * nblk + jt[p], 0, 0)),
                  cst(lq1), cst(lk1), cst(lq2), cst(lk2), cst(gs_col),
                  per_seq(N_MAPS, ATT_W), per_seq(1, ATT_W), per_seq(N_MAPS, V_DIM),
                  cst(slopes), cst(gs),
                  pl.BlockSpec(memory_space=pl.ANY), pl.BlockSpec(memory_space=pl.ANY)],
        out_specs=(pl.BlockSpec((tb, ATT_W), lambda b, p, it, jt, pt: (b * nblk + it[p], 0)),
                   per_seq(1, ATT_W)),
        scratch_shapes=[pltpu.VMEM((N_MAPS, 1, tb), F32),
                        pltpu.VMEM((N_MAPS, VE_ROWS, tb), F32),
                        pltpu.VMEM((tb, tb), F32),
                        pltpu.VMEM((2, pps, ATT_W, PAGE), F32),
                        pltpu.VMEM((2, pps, PAGE * N_HEADS, V_DIM), F32),
                        pltpu.SemaphoreType.DMA((2, 2)),
                        pltpu.VMEM((N_MAPS, 1), F32), pltpu.VMEM((N_MAPS, 1), F32),
                        pltpu.VMEM((N_MAPS, V_DIM), F32)],
    )
    return pl.pallas_call(
        kern,
        grid_spec=grid_spec,
        out_shape=(jax.ShapeDtypeStruct((batch * seq, ATT_W), BF16),
                   jax.ShapeDtypeStruct((nb, 1, ATT_W), F32)),
        compiler_params=_cparams(("arbitrary", "arbitrary")),
        name="fused_attn",
    )(i_tab, j_tab, pt_flat, qt, kbe, vte, lq1, lk1, lq2, lk2, gs_col,
      qblk, k_new, v_rows, slopes, gs, cache_kt, cache_v2)


def _expm1(y):
    acc = 1.0 + y * (1.0 / 10.0)
    for n in range(9, 1, -1):
        acc = 1.0 + (y * (1.0 / n)) * acc
    return jnp.where(jnp.abs(y) < 0.125, y * acc, jnp.exp(y) - 1.0)


def _log1p(z):
    return jnp.where(z < 1e-4, z * (1.0 - z * (0.5 - z * (1.0 / 3.0))), jnp.log(1.0 + z))


def _rg_gates(xc, wa_ref, ba_ref, wx_ref, bx_ref, lam_ref):
    xcb = xc.astype(BF16)
    r = _sigmoid(jnp.dot(xcb, wa_ref[...], preferred_element_type=F32) + ba_ref[...])
    ig = _sigmoid(jnp.dot(xcb, wx_ref[...], preferred_element_type=F32) + bx_ref[...])
    nl = -lam_ref[...]
    sp = jnp.maximum(nl, 0.0) + _log1p(jnp.exp(-jnp.abs(nl)))
    log_a = -RG_C * r * sp
    a = jnp.exp(log_a)
    bt = jnp.sqrt(-_expm1(2.0 * log_a)) * (ig * xc)
    return a, bt


def _rglru_seq_kernel(xr_ref, gr_ref, cw_ref, cb_ref, wa_ref, ba_ref, wx_ref, bx_ref, lam_ref,
                      rg_ref, ctail_ref, htail_ref, prev_scr, h_scr, *, ts):
    s = pl.program_id(1)

    @pl.when(s == 0)
    def _():
        prev_scr[...] = jnp.zeros(prev_scr.shape, F32)
        h_scr[...] = jnp.zeros(h_scr.shape, F32)

    x = xr_ref[...]
    prev = prev_scr[...]
    row8 = lax.broadcasted_iota(jnp.int32, (SUBLANES, 1), 0)
    cw = cw_ref[...]
    xc = cb_ref[...] + x * cw[CONV_W - 1:CONV_W]
    for jj in range(1, CONV_W):
        xs = pltpu.roll(x, jj, axis=0)
        ps = pltpu.roll(prev, jj, axis=0)
        top = jnp.where(row8 < jj, ps, xs[:SUBLANES])
        xs = jnp.concatenate([top, xs[SUBLANES:]], axis=0)
        xc = xc + xs * cw[CONV_W - 1 - jj:CONV_W - jj]
    a, bt = _rg_gates(xc, wa_ref, ba_ref, wx_ref, bx_ref, lam_ref)
    grp = min(32, ts)
    row_in_grp = lax.broadcasted_iota(jnp.int32, (ts, 1), 0) % grp
    d = 1
    while d < grp:
        a_s = pltpu.roll(a, d, axis=0)
        b_s = pltpu.roll(bt, d, axis=0)
        keep = row_in_grp >= d
        bt = jnp.where(keep, a * b_s + bt, bt)
        a = jnp.where(keep, a * a_s, a)
        d *= 2
    carry = h_scr[...]
    parts = []
    for g0 in range(0, ts, grp):
        hg = a[g0:g0 + grp] * carry + bt[g0:g0 + grp]
        parts.append(hg)
        carry = hg[grp - 1:grp]
    h = jnp.concatenate(parts, axis=0)
    rg_ref[...] = (h * jax.nn.gelu(gr_ref[...])).astype(rg_ref.dtype)
    h_scr[...] = h[ts - 1:ts]
    prev_scr[...] = x[ts - SUBLANES:ts]
    ctail_ref[...] = x[ts - SUBLANES:ts]
    htail_ref[...] = h[ts - SUBLANES:ts]


def _rglru_seq(xr, gr, cw, cb, wa_bd, ba, wx_bd, bx, lam, batch, seq):
    ts = min(256, seq)
    ns = seq // ts
    kern = functools.partial(_rglru_seq_kernel, ts=ts)
    cst = lambda a: pl.BlockSpec(a.shape, lambda b, s: (0,) * a.ndim)
    tile = pl.BlockSpec((ts, RG_W), lambda b, s: (b * ns + s, 0))
    tail = pl.BlockSpec((None, SUBLANES, RG_W), lambda b, s: (b, 0, 0))
    return pl.pallas_call(
        kern,
        grid=(batch, ns),
        in_specs=[tile, tile, cst(cw), cst(cb), cst(wa_bd), cst(ba), cst(wx_bd), cst(bx), cst(lam)],
        out_specs=(tile, tail, tail),
        out_shape=(jax.ShapeDtypeStruct((batch * seq, RG_W), BF16),
                   jax.ShapeDtypeStruct((batch, SUBLANES, RG_W), F32),
                   jax.ShapeDtypeStruct((batch, SUBLANES, RG_W), F32)),
        scratch_shapes=[pltpu.VMEM((SUBLANES, RG_W), F32), pltpu.VMEM((1, RG_W), F32)],
        compiler_params=_cparams(("parallel", "arbitrary")),
        name="rglru_seq",
    )(xr, gr, cw, cb, wa_bd, ba, wx_bd, bx, lam)


def _rglru_step_kernel(xr_ref, gr_ref, c0_ref, c1_ref, c2_ref, h0_ref, cw_ref, cb_ref, wa_ref, ba_ref,
                       wx_ref, bx_ref, lam_ref, rg_ref, h_ref):
    x = xr_ref[...]
    cw = cw_ref[...]
    xc = cb_ref[...] + c0_ref[...] * cw[0:1]
    xc = xc + c1_ref[...] * cw[1:2]
    xc = xc + c2_ref[...] * cw[2:3]
    xc = xc + x * cw[3:4]
    a, bt = _rg_gates(xc, wa_ref, ba_ref, wx_ref, bx_ref, lam_ref)
    h = a * h0_ref[...] + bt
    h_ref[...] = h
    rg_ref[...] = (h * jax.nn.gelu(gr_ref[...])).astype(rg_ref.dtype)


def _rglru_step(xr, gr, conv_state, h0, cw, cb, wa_bd, ba, wx_bd, bx, lam):
    nb = xr.shape[0]
    args = (xr, gr, conv_state[:, 0], conv_state[:, 1], conv_state[:, 2], h0,
            cw, cb, wa_bd, ba, wx_bd, bx, lam)
    return pl.pallas_call(
        _rglru_step_kernel,
        grid=(1,),
        in_specs=[_full(a.shape) for a in args],
        out_specs=(_full((nb, RG_W)), _full((nb, RG_W))),
        out_shape=(jax.ShapeDtypeStruct((nb, RG_W), BF16), jax.ShapeDtypeStruct((nb, RG_W), F32)),
        compiler_params=_cparams(("arbitrary",)),
        name="rglru_step",
    )(*args)


def _outproj_router_kernel(att_ref, rg_ref, x_ref, wo_ref, g2_ref, wr_ref, rb_ref, tri_ref,
                           x1_ref, hm8_ref, hmb_ref, idx_ref, gate_ref, rank_ref, cnt_ref,
                           carry_scr, *, tm, tiles_per_super):
    i = pl.program_id(0)

    @pl.when(i % tiles_per_super == 0)
    def _():
        carry_scr[...] = jnp.zeros(carry_scr.shape, F32)

    mixed = (jnp.dot(att_ref[...], wo_ref[:ATT_W, :], preferred_element_type=F32)
             + jnp.dot(rg_ref[...], wo_ref[ATT_W:, :], preferred_element_type=F32))
    x1 = x_ref[...] + mixed
    x1_ref[...] = x1
    ms = jnp.mean(x1 * x1, axis=-1, keepdims=True)
    hm = x1 * lax.rsqrt(ms + EPS) * g2_ref[...]
    hmb = hm.astype(BF16)
    hmb_ref[...] = hmb
    for s in range(SUBLANES):
        hm8_ref[pl.ds(s, tm, stride=SUBLANES), :] = hm[:, s * LANES:(s + 1) * LANES]

    scores = _sigmoid(_nt_dot(wr_ref[...], hmb))
    choice = scores + rb_ref[...]
    ch3 = choice.reshape(N_GROUPS, GROUP_SIZE, tm)
    i8 = lax.broadcasted_iota(jnp.int32, (N_GROUPS, GROUP_SIZE, tm), 1)
    m1 = jnp.max(ch3, axis=1, keepdims=True)
    f1 = jnp.min(jnp.where(ch3 == m1, i8, GROUP_SIZE), axis=1, keepdims=True)
    m2 = jnp.max(jnp.where(i8 == f1, -jnp.inf, ch3), axis=1, keepdims=True)
    gs3 = m1 + m2
    gs = gs3.reshape(N_GROUPS, tm)
    gi = lax.broadcasted_iota(jnp.int32, (N_GROUPS, tm), 0)
    grank = jnp.zeros((N_GROUPS, tm), jnp.int32)
    for g2 in range(N_GROUPS):
        rowv = gs[g2:g2 + 1, :]
        beats = (rowv > gs) | ((rowv == gs) & (g2 < gi))
        grank = grank + beats.astype(jnp.int32)
    gsel = (grank < TOPK_GROUPS).reshape(N_GROUPS, 1, tm)
    masked = jnp.where(gsel, ch3, NEG_INF).reshape(N_EXPERTS, tm)

    ei = lax.broadcasted_iota(jnp.int32, (N_EXPERTS, tm), 0)
    picks, gvals = [], []
    sel = jnp.zeros((N_EXPERTS, tm), F32)
    for _ in range(TOP_K):
        mx = jnp.max(masked, axis=0, keepdims=True)
        first = jnp.min(jnp.where(masked == mx, ei, N_EXPERTS), axis=0, keepdims=True)
        hit = ei == first
        picks.append(first)
        gvals.append(jnp.sum(jnp.where(hit, scores, 0.0), axis=0, keepdims=True))
        sel = jnp.where(hit, 1.0, sel)
        masked = jnp.where(hit, -jnp.inf, masked)
    gsum = gvals[0]
    for kk in range(1, TOP_K):
        gsum = gsum + gvals[kk]

    pos = jnp.dot(sel.astype(BF16), tri_ref[...], preferred_element_type=F32) + carry_scr[...]
    for kk in range(TOP_K):
        idx_ref[kk:kk + 1, :] = picks[kk]
        gate_ref[kk:kk + 1, :] = gvals[kk] / gsum * ROUTE_SCALE
        rk = jnp.sum(jnp.where(ei == picks[kk], pos, 0.0), axis=0, keepdims=True)
        rank_ref[kk:kk + 1, :] = rk.astype(jnp.int32)
    carry = carry_scr[...] + jnp.sum(sel, axis=1, keepdims=True)
    carry_scr[...] = carry
    cnt_ref[...] = jnp.broadcast_to(carry, cnt_ref.shape)


def _outproj_router(att, rg, x2, wo_bf, g2, wr_t, rb_col, tt):
    T, D = x2.shape
    tm = min(512, T)
    tiles_per_super = tt // tm
    n_super = T // tt
    tri = jnp.triu(jnp.ones((tm, tm), BF16), k=1)
    kern = functools.partial(_outproj_router_kernel, tm=tm, tiles_per_super=tiles_per_super)
    row = lambda w: pl.BlockSpec((tm, w), lambda i: (i, 0))
    col = pl.BlockSpec((TOP_K, tm), lambda i: (0, i))
    return pl.pallas_call(
        kern,
        grid=(T // tm,),
        in_specs=[row(ATT_W), row(RG_W), row(D), _full(wo_bf.shape), _full(g2.shape),
                  _full(wr_t.shape), _full(rb_col.shape), _full(tri.shape)],
        out_specs=(row(D), pl.BlockSpec((tm * SUBLANES, LANES), lambda i: (i, 0)), row(D),
                   col, col, col,
                   pl.BlockSpec((None, N_EXPERTS, LANES), lambda i: (i // tiles_per_super, 0, 0))),
        out_shape=(jax.ShapeDtypeStruct((T, D), F32),
                   jax.ShapeDtypeStruct((T * SUBLANES, LANES), F32),
                   jax.ShapeDtypeStruct((T, D), BF16),
                   jax.ShapeDtypeStruct((TOP_K, T), jnp.int32),
                   jax.ShapeDtypeStruct((TOP_K, T), F32),
                   jax.ShapeDtypeStruct((TOP_K, T), jnp.int32),
                   jax.ShapeDtypeStruct((n_super, N_EXPERTS, LANES), F32)),
        scratch_shapes=[pltpu.VMEM((N_EXPERTS, 1), F32)],
        compiler_params=_cparams(("arbitrary",)),
        name="outproj_router",
    )(att, rg, x2, wo_bf, g2, wr_t, rb_col, tri)


def _moe_kernel(cnt_ref, off_ref, hm8_ref, w13_ref, w2_ref, idx_ref, rank_ref, gate_ref, y8_ref,
                dest_v, dest_s, gate_s, row_s, gsort_s, xg, og, sem, *, tt, mch, d_model, d_exp):
    ti = pl.program_id(0)
    e = pl.program_id(1)
    chunks = d_model // LANES

    def rows_at(off):
        return pl.ds(pl.multiple_of(off, chunks), chunks)

    @pl.when(e == 0)
    def _():
        idx = idx_ref[...]
        dest = rank_ref[...]
        for ee in range(N_EXPERTS):
            dest = dest + jnp.where(idx == ee, off_ref[ti * N_EXPERTS + ee], 0)
        dest_v[...] = dest
        copies = []
        for kk in range(TOP_K):
            span = pl.ds(kk * tt, tt)
            copies.append(pltpu.make_async_copy(dest_v.at[kk], dest_s.at[span], sem.at[0]))
            copies.append(pltpu.make_async_copy(gate_ref.at[kk], gate_s.at[span], sem.at[1]))
        for cp in copies:
            cp.start()
        y8_ref[...] = jnp.zeros(y8_ref.shape, F32)
        xg[...] = jnp.zeros(xg.shape, F32)
        og[...] = jnp.zeros(og.shape, F32)
        for cp in copies:
            cp.wait()

        def invert(io, _):
            for u in range(SUBLANES):
                t = io * SUBLANES + u
                picks = [(dest_s[kk * tt + t], gate_s[kk * tt + t]) for kk in range(TOP_K)]
                for slot, gt in picks:
                    row_s[slot] = t * chunks
                    gsort_s[slot] = gt
            return 0

        lax.fori_loop(0, tt // SUBLANES, invert, 0)

    c = cnt_ref[ti * N_EXPERTS + e]
    off = off_ref[ti * N_EXPERTS + e]
    nch = (c + mch - 1) // mch
    g_unroll = 32
    s_unroll = 8

    def chunk(ch, carry):
        base = off + ch * mch
        nvalid = jnp.minimum(c - ch * mch, mch)

        def gather_row(ii):
            xg[rows_at(ii * chunks), :] = hm8_ref[rows_at(row_s[base + ii]), :]

        def gather_group(io, _):
            for u in range(g_unroll):
                gather_row(io * g_unroll + u)
            return 0

        def gather_one(ii, _):
            gather_row(ii)
            return 0

        n_full = nvalid // g_unroll
        lax.fori_loop(0, n_full, gather_group, 0)
        lax.fori_loop(n_full * g_unroll, nvalid, gather_one, 0)
        xb = jnp.concatenate([xg[pl.ds(s, mch, stride=chunks), :] for s in range(chunks)],
                             axis=-1).astype(BF16)
        h13 = jnp.dot(xb, w13_ref[...], preferred_element_type=F32)
        h1 = h13[:, :d_exp]
        hdn = (h1 * _sigmoid(h1)) * h13[:, d_exp:]
        o = jnp.dot(hdn.astype(BF16), w2_ref[...], preferred_element_type=F32)
        for s in range(chunks):
            og[pl.ds(s, mch, stride=chunks), :] = o[:, s * LANES:(s + 1) * LANES]

        def scatter_rows(first, n):
            vals = []
            for u in range(n):
                ii = first + u
                r = row_s[base + ii]
                gt = gsort_s[base + ii]
                vals.append((r, y8_ref[rows_at(r), :] + gt * og[rows_at(ii * chunks), :]))
            for r, v in vals:
                y8_ref[rows_at(r), :] = v

        def scatter_group(io, _):
            scatter_rows(io * s_unroll, s_unroll)
            return 0

        def scatter_one(ii, _):
            scatter_rows(ii, 1)
            return 0

        n_full_s = nvalid // s_unroll
        lax.fori_loop(0, n_full_s, scatter_group, 0)
        lax.fori_loop(n_full_s * s_unroll, nvalid, scatter_one, 0)
        return carry

    lax.fori_loop(0, nch, chunk, 0)


def _moe(hm8, w13, w2, idx, rank, gate, cnt_flat, off_flat, tt, mch):
    T8, _ = hm8.shape
    T = T8 // SUBLANES
    n_tiles = T // tt
    d_model = w13.shape[1]
    d_exp = w2.shape[1]
    assert d_model == SUBLANES * LANES and tt & (tt - 1) == 0 and mch % SUBLANES == 0
    kern = functools.partial(_moe_kernel, tt=tt, mch=mch, d_model=d_model, d_exp=d_exp)
    single = dict(pipeline_mode=pl.Buffered(1))
    picks = pl.BlockSpec((TOP_K, tt), lambda t, e, c, o: (0, t))
    grid_spec = pltpu.PrefetchScalarGridSpec(
        num_scalar_prefetch=2,
        grid=(n_tiles, N_EXPERTS),
        in_specs=[pl.BlockSpec((tt * SUBLANES, LANES), lambda t, e, c, o: (t, 0), **single),
                  pl.BlockSpec((None, d_model, 2 * d_exp), lambda t, e, c, o: (e, 0, 0)),
                  pl.BlockSpec((None, d_exp, d_model), lambda t, e, c, o: (e, 0, 0)),
                  picks, picks, picks],
        out_specs=pl.BlockSpec((tt * SUBLANES, LANES), lambda t, e, c, o: (t, 0), **single),
        scratch_shapes=[pltpu.VMEM((TOP_K, tt), jnp.int32),
                        pltpu.SMEM((tt * TOP_K,), jnp.int32), pltpu.SMEM((tt * TOP_K,), F32),
                        pltpu.SMEM((tt * TOP_K,), jnp.int32), pltpu.SMEM((tt * TOP_K,), F32),
                        pltpu.VMEM((mch * SUBLANES, LANES), F32),
                        pltpu.VMEM((mch * SUBLANES, LANES), F32),
                        pltpu.SemaphoreType.DMA((2,))],
    )
    return pl.pallas_call(
        kern,
        grid_spec=grid_spec,
        out_shape=jax.ShapeDtypeStruct((T8, LANES), F32),
        compiler_params=_cparams(("arbitrary", "arbitrary")),
        name="moe_experts",
    )(cnt_flat, off_flat, hm8, w13, w2, idx, rank, gate)


def _final_kernel(y8_ref, x1_ref, hmb_ref, ws13_ref, ws2_ref, gp_ref, wg_ref, p_ref, wp_ref, o_ref,
                  *, tm, d_sh):
    chunks = x1_ref.shape[1] // LANES
    routed = jnp.concatenate([y8_ref[pl.ds(s, tm, stride=chunks), :] for s in range(chunks)], axis=-1)
    hmb = hmb_ref[...]
    h13 = jnp.dot(hmb, ws13_ref[...], preferred_element_type=F32)
    h1 = h13[:, :d_sh]
    hdn = (h1 * _sigmoid(h1)) * h13[:, d_sh:]
    shared = jnp.dot(hdn.astype(BF16), ws2_ref[...], preferred_element_type=F32)
    x2 = x1_ref[...] + (routed + shared)
    ms = jnp.mean(x2 * x2, axis=-1, keepdims=True)
    hp = (x2 * lax.rsqrt(ms + EPS) * gp_ref[...]).astype(BF16)
    gate = _sigmoid(jnp.dot(hp, wg_ref[...], preferred_element_type=F32))
    proj = jnp.dot(p_ref[...].astype(BF16), wp_ref[...], preferred_element_type=F32)
    o_ref[...] = x2 + gate * proj


def _final(y8, x1, hmb, ws13, ws2, gp, wg, p2, wp):
    T, D = x1.shape
    tm = min(512, T)
    d_sh = ws2.shape[0]
    kern = functools.partial(_final_kernel, tm=tm, d_sh=d_sh)
    row = lambda w: pl.BlockSpec((tm, w), lambda i: (i, 0))
    return pl.pallas_call(
        kern,
        grid=(T // tm,),
        in_specs=[pl.BlockSpec((tm * SUBLANES, LANES), lambda i: (i, 0)), row(D), row(D),
                  _full(ws13.shape), _full(ws2.shape), _full(gp.shape), _full(wg.shape),
                  row(p2.shape[1]), _full(wp.shape)],
        out_specs=row(D),
        out_shape=jax.ShapeDtypeStruct((T, D), F32),
        compiler_params=_cparams(("parallel",)),
        name="final",
    )(y8, x1, hmb, ws13, ws2, gp, wg, p2, wp)


def _block_diag(w):
    nh, d, _ = w.shape
    eye = jnp.eye(nh, dtype=w.dtype)
    return (eye[:, None, :, None] * w[:, :, None, :]).reshape(nh * d, nh * d)


def _prep_weights(i, g_norm1, w_in, g_q, g_k, lam_q1, lam_k1, lam_q2, lam_k2, g_subln, conv_w, conv_b,
                  w_a, b_a, w_x, b_x, lam_rg, w_out, g_norm2, w_router, router_bias, w_e1, w_e3, w_e2,
                  w_s1, w_s3, w_s2, g_ple, w_ple_gate, w_ple_proj):
    return dict(
        g1=g_norm1[i][None], w_in=w_in[i].astype(BF16),
        wqt=w_in[i][:, :ATT_W].T.astype(BF16), wkt=w_in[i][:, ATT_W:2 * ATT_W].T.astype(BF16),
        wvt=w_in[i][:, 2 * ATT_W:3 * ATT_W].T.astype(BF16),
        gq_col=g_q[i][:, None], gk_col=g_k[i][:, None],
        lq1=lam_q1[i][None], lk1=lam_k1[i][None], lq2=lam_q2[i][None], lk2=lam_k2[i][None],
        gs=g_subln[i][None], gs_col=g_subln[i][:, None],
        cw=conv_w[i], cb=conv_b[i][None],
        wa=_block_diag(w_a[i]).astype(BF16), ba=b_a[i][None],
        wx=_block_diag(w_x[i]).astype(BF16), bx=b_x[i][None], lam_rg=lam_rg[i][None],
        wo=w_out[i].astype(BF16), g2=g_norm2[i][None],
        wr_t=w_router[i].T.astype(BF16), rb=router_bias[i][:, None],
        w13=jnp.concatenate([w_e1[i], w_e3[i]], axis=-1).astype(BF16), w2=w_e2[i].astype(BF16),
        ws13=jnp.concatenate([w_s1[i], w_s3[i]], axis=-1).astype(BF16), ws2=w_s2[i].astype(BF16),
        gp=g_ple[i][None], wg=w_ple_gate[i].astype(BF16), wp=w_ple_proj[i].astype(BF16),
    )


def _moe_stage(att, rg, x2, p2, W, tt, mch):
    T = x2.shape[0]
    x1, hm8, hmb, idx, gate, rank, cnt = _outproj_router(att, rg, x2, W['wo'], W['g2'], W['wr_t'],
                                                         W['rb'], tt)
    cnt_i = cnt[:, :, 0].astype(jnp.int32)
    off_i = jnp.cumsum(cnt_i, axis=1) - cnt_i
    y8 = _moe(hm8, W['w13'], W['w2'], idx, rank, gate, cnt_i.reshape(-1), off_i.reshape(-1), tt, mch)
    return _final(y8, x1, hmb, W['ws13'], W['ws2'], W['gp'], W['wg'], p2, W['wp'])


def _slopes_col():
    h = jnp.arange(N_MAPS) // 2 + 1
    return (LOG2E * jnp.exp2(-8.0 * h.astype(F32) / N_HEADS))[:, None]


def kernel(x_prompt, x_sample, cache_k, cache_v, state_conv, state_h, page_table, p_prompt, p_sample, g_norm1, w_in, g_q, g_k, lam_q1, lam_k1, lam_q2, lam_k2, g_subln, conv_w, conv_b, w_a, b_a, w_x, b_x, lam_rg, w_out, g_norm2, w_router, router_bias, w_e1, w_e3, w_e2, w_s1, w_s3, w_s2, g_ple, w_ple_gate, w_ple_proj):
    depth = w_in.shape[0]
    B, S, D = x_prompt.shape
    NB, NS, _ = x_sample.shape
    assert NS == 1
    n_pool = cache_k.shape[1]
    yp = x_prompt.reshape(B * S, D)
    ys = x_sample.reshape(NB * NS, D)
    kp_l, vp_l, cp_l, hp_l, ks_l, vs_l, cs_l, hs_l = [], [], [], [], [], [], [], []
    for i in range(depth):
        W = _prep_weights(i, g_norm1, w_in, g_q, g_k, lam_q1, lam_k1, lam_q2, lam_k2, g_subln, conv_w,
                          conv_b, w_a, b_a, w_x, b_x, lam_rg, w_out, g_norm2, w_router, router_bias,
                          w_e1, w_e3, w_e2, w_s1, w_s3, w_s2, g_ple, w_ple_gate, w_ple_proj)
        lam_init = 0.8 - 0.6 * math.exp(-0.3 * i)
        lam_vecs = (W['lq1'], W['lk1'], W['lq2'], W['lk2'])
        inproj_w = (W['g1'], W['w_in'], W['wqt'], W['wkt'], W['wvt'], W['gq_col'], W['gk_col'])

        tb = min(512, S)
        qt, kt, kbe, v4, vte, xr, gr = _inproj(yp, B, tb, *inproj_w)
        qt_s, kt_s, _, v4_s, _, xr_s, gr_s = _inproj(ys, 1, PAGE, *inproj_w)
        k_s = kt_s[0].T
        v_s = v4_s.reshape(NB, N_HEADS, V_DIM)
        qm = qt_s[0].T.reshape(NB, N_MAPS, V_DIM)[:, :, :HEAD_DIM]
        msel = jnp.eye(N_MAPS, dtype=BF16)
        qblk = (qm[:, :, None, :] * msel[None, :, :, None]).reshape(NB, N_MAPS, ATT_W)
        v_rows = jnp.repeat(v_s, 2, axis=1)
        cache_kt = jnp.transpose(cache_k[i], (0, 2, 3, 4, 1)).reshape(n_pool, ATT_W, PAGE)
        cache_v2 = cache_v[i].reshape(n_pool, PAGE * N_HEADS, V_DIM)
        decode_args = (qblk, k_s.reshape(NB, 1, ATT_W), v_rows, _slopes_col())

        nblk = S // tb
        pps = _fused_decode_pages_per_step(B * (nblk * (nblk + 1) // 2), NB, page_table.shape[1])
        if pps is not None and pps <= 32:
            att, att_s = _fused_attention(qt, kbe, vte, *lam_vecs, W['gs_col'], B, S, tb,
                                          *decode_args, W['gs'], cache_kt, cache_v2, page_table, pps,
                                          lam_init)
        else:
            att = _prompt_attention(qt, kbe, vte, *lam_vecs, W['gs_col'], B, S, tb, lam_init)
            att_s = _decode_attention(*decode_args, *lam_vecs, W['gs'], cache_kt, cache_v2,
                                      page_table, lam_init)

        rg, ctail, htail = _rglru_seq(xr, gr, W['cw'], W['cb'], W['wa'], W['ba'], W['wx'], W['bx'],
                                      W['lam_rg'], B, S)
        tt_p = min(4096, B * S)
        mch_p = tt_p // SUBLANES + tt_p // 64
        yp = _moe_stage(att, rg, yp, p_prompt[i].reshape(B * S, -1), W, tt_p, mch_p)
        kp_l.append(jnp.transpose(kt.reshape(B, N_HEADS, 2, HEAD_DIM, S), (0, 4, 1, 2, 3)))
        vp_l.append(v4.reshape(B, S, N_HEADS, V_DIM))
        cp_l.append(ctail[:, SUBLANES - (CONV_W - 1):])
        hp_l.append(htail[:, SUBLANES - 1])

        rg_s, h_s = _rglru_step(xr_s, gr_s, state_conv[i], state_h[i], W['cw'], W['cb'], W['wa'],
                                W['ba'], W['wx'], W['bx'], W['lam_rg'])
        ys = _moe_stage(att_s.reshape(NB, ATT_W).astype(BF16), rg_s, ys,
                        p_sample[i].reshape(NB * NS, -1), W, NB, 32)
        ks_l.append(k_s.reshape(NB, NS, N_HEADS, 2, HEAD_DIM))
        vs_l.append(v_s.reshape(NB, NS, N_HEADS, V_DIM))
        cs_l.append(jnp.concatenate([state_conv[i][:, 1:], xr_s[:, None, :]], axis=1))
        hs_l.append(h_s)
    return (yp.reshape(B, S, D), ys.reshape(NB, NS, D), jnp.stack(kp_l), jnp.stack(vp_l),
            jnp.stack(cp_l), jnp.stack(hp_l), jnp.stack(ks_l), jnp.stack(vs_l), jnp.stack(cs_l),
            jnp.stack(hs_l))
```

```python
import functools
import math

import jax
import jax.numpy as jnp
from jax import lax
from jax.experimental import pallas as pl
from jax.experimental.pallas import tpu as pltpu

F32 = jnp.float32
BF16 = jnp.bfloat16

EPS = 1e-6
NEG_INF = -1e30
LANES = 128
SUBLANES = 8
N_HEADS = 4
HEAD_DIM = 64
V_DIM = 2 * HEAD_DIM
N_MAPS = 2 * N_HEADS
ATT_W = N_HEADS * V_DIM
RG_W = 512
RG_C = 8.0
CONV_W = 4
N_EXPERTS = 64
N_GROUPS = 8
GROUP_SIZE = N_EXPERTS // N_GROUPS
TOPK_GROUPS = 4
TOP_K = 8
ROUTE_SCALE = 2.5
ATT_SCALE = HEAD_DIM ** -0.5
LOG2E = math.log2(math.e)
VE_ROWS = V_DIM + 16
PAGE = 128
VMEM_LIMIT = 56 * 1024 * 1024


def _cparams(sem):
    return pltpu.CompilerParams(dimension_semantics=sem, vmem_limit_bytes=VMEM_LIMIT)


def _sigmoid(x):
    return 1.0 / (1.0 + jnp.exp(-x))


def _nt_dot(a, b):
    return lax.dot_general(a, b, (((1,), (1,)), ((), ())), preferred_element_type=F32)


def _full(shape):
    n = len(shape)
    return pl.BlockSpec(shape, lambda *_: (0,) * n)


def _alibi_slopes2():
    return [LOG2E * 2.0 ** (-8.0 * (hh + 1) / N_HEADS) for hh in range(N_HEADS)]


def _inproj_kernel(x_ref, g1_ref, w_ref, wqt_ref, wkt_ref, wvt_ref, gq_ref, gk_ref,
                   qt_ref, kt_ref, kbe_ref, v4_ref, vte_ref, xr_ref, gr_ref, *, tb):
    x = x_ref[...]
    tm = x.shape[0]
    ms = jnp.mean(x * x, axis=-1, keepdims=True)
    h = (x * lax.rsqrt(ms + EPS) * g1_ref[...]).astype(BF16)

    def proj(lo, hi):
        return jnp.dot(h, w_ref[:, lo:hi], preferred_element_type=F32)

    def feature_major_normed(wt_ref, g_ref):
        z = _nt_dot(wt_ref[...], h).reshape(N_MAPS, HEAD_DIM, tm)
        zms = jnp.mean(z * z, axis=1, keepdims=True)
        return z * lax.rsqrt(zms + EPS) * g_ref[...][None]

    qn = feature_major_normed(wqt_ref, gq_ref) * (ATT_SCALE * LOG2E)
    kn = feature_major_normed(wkt_ref, gk_ref)
    kt_ref[...] = kn.reshape(N_MAPS * HEAD_DIM, tm)

    row8 = lax.broadcasted_iota(jnp.int32, (SUBLANES, tm), 0)
    pad = jnp.zeros((V_DIM - HEAD_DIM - SUBLANES, tm), F32)
    q_tail = jnp.concatenate([jnp.where(row8 < 3, 1.0, 0.0), pad], axis=0).astype(BF16)
    tok = pl.program_id(0) * tm + lax.broadcasted_iota(jnp.int32, (1, tm), 1)
    k_local = (tok % tb).astype(F32)
    slopes2 = _alibi_slopes2()
    kte = []
    for mm in range(N_MAPS):
        qt_ref[mm * V_DIM:mm * V_DIM + HEAD_DIM, :] = qn[mm].astype(BF16)
        qt_ref[mm * V_DIM + HEAD_DIM:(mm + 1) * V_DIM, :] = q_tail
        val = slopes2[mm // 2] * k_local
        hi = val.astype(BF16).astype(F32)
        mid = (val - hi).astype(BF16).astype(F32)
        lo = ((val - hi) - mid).astype(BF16).astype(F32)
        k_tail = jnp.where(row8 == 0, hi, jnp.where(row8 == 1, mid, jnp.where(row8 == 2, lo, 0.0)))
        kte += [kn[mm], k_tail, pad]
    kbe_ref[...] = jnp.concatenate(kte, axis=0).T.astype(BF16)

    vt = _nt_dot(wvt_ref[...], h)
    ones = jnp.ones((VE_ROWS - V_DIM, tm), BF16)
    for hh in range(N_HEADS):
        vh = vt[hh * V_DIM:(hh + 1) * V_DIM]
        v4_ref[pl.ds(hh, tm, stride=N_HEADS), :] = vh.T
        vte_ref[hh * VE_ROWS:hh * VE_ROWS + V_DIM, :] = vh.astype(BF16)
        vte_ref[hh * VE_ROWS + V_DIM:(hh + 1) * VE_ROWS, :] = ones
    xr_ref[...] = proj(3 * ATT_W, 3 * ATT_W + RG_W)
    gr_ref[...] = proj(3 * ATT_W + RG_W, 3 * ATT_W + 2 * RG_W)


def _inproj(x2, batch, tb, g1, w_in_bf, wqt, wkt, wvt, gq_col, gk_col):
    T, D = x2.shape
    seq = T // batch
    tm = min(512, seq)
    tiles = seq // tm
    row = lambda w: pl.BlockSpec((tm, w), lambda i: (i, 0))
    col = lambda r: pl.BlockSpec((None, r, tm), lambda i: (i, 0, 0))
    out_shape = (
        jax.ShapeDtypeStruct((T // tm, N_MAPS * V_DIM, tm), BF16),
        jax.ShapeDtypeStruct((batch, ATT_W, seq), F32),
        jax.ShapeDtypeStruct((T, N_MAPS * V_DIM), BF16),
        jax.ShapeDtypeStruct((T * N_HEADS, V_DIM), F32),
        jax.ShapeDtypeStruct((T // tm, N_HEADS * VE_ROWS, tm), BF16),
        jax.ShapeDtypeStruct((T, RG_W), F32), jax.ShapeDtypeStruct((T, RG_W), F32),
    )
    args = (x2, g1, w_in_bf, wqt, wkt, wvt, gq_col, gk_col)
    return pl.pallas_call(
        functools.partial(_inproj_kernel, tb=tb),
        grid=(T // tm,),
        in_specs=[row(D)] + [_full(a.shape) for a in args[1:]],
        out_specs=(col(N_MAPS * V_DIM),
                   pl.BlockSpec((None, ATT_W, tm), lambda i: (i // tiles, 0, i % tiles)),
                   row(N_MAPS * V_DIM),
                   pl.BlockSpec((tm * N_HEADS, V_DIM), lambda i: (i, 0)),
                   col(N_HEADS * VE_ROWS), row(RG_W), row(RG_W)),
        out_shape=out_shape,
        compiler_params=_cparams(("parallel",)),
        name="inproj",
    )(*args)


def _lam_value(lq1, lk1, lq2, lk2, lam_init):
    s1 = jnp.sum(lq1 * lk1, axis=-1, keepdims=True)
    s2 = jnp.sum(lq2 * lk2, axis=-1, keepdims=True)
    return jnp.exp(s1) - jnp.exp(s2) + lam_init


def _subln(o, g, lam_init):
    ms = jnp.mean(o * o, axis=-1, keepdims=True)
    return (o * lax.rsqrt(ms + EPS) * g) * (1.0 - lam_init)


def _prompt_attn_step(pidx, it_ref, jt_ref, qt_ref, kbe_ref, vte_ref, lq1_ref, lk1_ref, lq2_ref,
                      lk2_ref, gsc_ref, o_ref, m_scr, acc_scr, mask_scr, *, tb, strip, lam_init):
    i = it_ref[pidx]
    j = jt_ref[pidx]
    slopes2 = _alibi_slopes2()

    @pl.when(pidx == 0)
    def _():
        kl = lax.broadcasted_iota(jnp.int32, (tb, tb), 0)
        ql = lax.broadcasted_iota(jnp.int32, (tb, tb), 1)
        mask_scr[...] = jnp.where(kl <= ql, 0.0, NEG_INF)

    @pl.when(j == 0)
    def _():
        m_scr[...] = jnp.full(m_scr.shape, NEG_INF, F32)
        acc_scr[...] = jnp.zeros(acc_scr.shape, F32)

    gap = ((i - j) * tb).astype(F32)

    def update(diagonal):
        for mm in range(N_MAPS):
            hh = mm // 2
            off = -slopes2[hh] * gap
            for s0 in range(0, tb, strip):
                nk = s0 + strip if diagonal else tb
                lanes = slice(s0, s0 + strip)
                u = jnp.dot(kbe_ref[:nk, mm * V_DIM:(mm + 1) * V_DIM],
                            qt_ref[mm * V_DIM:(mm + 1) * V_DIM, lanes], preferred_element_type=F32)
                if diagonal:
                    u = u + mask_scr[:nk, lanes]
                m_prev = m_scr[mm, :, lanes]
                m_new = jnp.maximum(m_prev, jnp.max(u, axis=0, keepdims=True) + off)
                alpha = jnp.exp2(m_prev - m_new)
                p = jnp.exp2(u - (m_new - off)).astype(BF16)
                acc_scr[mm, :, lanes] = alpha * acc_scr[mm, :, lanes] + jnp.dot(
                    vte_ref[hh * VE_ROWS:(hh + 1) * VE_ROWS, :nk], p, preferred_element_type=F32)
                m_scr[mm, :, lanes] = m_new

    @pl.when(j < i)
    def _():
        update(False)

    @pl.when(j == i)
    def _():
        update(True)
        lam = _lam_value(lq1_ref[...], lk1_ref[...], lq2_ref[...], lk2_ref[...], lam_init)
        for hh in range(N_HEADS):
            a1 = acc_scr[2 * hh]
            a2 = acc_scr[2 * hh + 1]
            o1 = a1[:V_DIM] / a1[V_DIM:V_DIM + 1]
            o2 = a2[:V_DIM] / a2[V_DIM:V_DIM + 1]
            o = o1 - lam * o2
            ms = jnp.mean(o * o, axis=0, keepdims=True)
            y = (o * lax.rsqrt(ms + EPS) * gsc_ref[...]) * (1.0 - lam_init)
            o_ref[:, hh * V_DIM:(hh + 1) * V_DIM] = y.T.astype(o_ref.dtype)


def _prompt_attn_kernel(*refs, **static):
    _prompt_attn_step(pl.program_id(1), *refs, **static)


def _attn_pairs(nb):
    pairs = [(i, j) for i in range(nb) for j in range(i + 1)]
    return (jnp.asarray([p[0] for p in pairs], jnp.int32),
            jnp.asarray([p[1] for p in pairs], jnp.int32), len(pairs))


def _prompt_attention(qt, kbe, vte, lq1, lk1, lq2, lk2, gs_col, batch, seq, tb, lam_init):
    nb = seq // tb
    strip = tb
    i_tab, j_tab, n_pairs = _attn_pairs(nb)
    cst = lambda a: pl.BlockSpec(a.shape, lambda b, p, it, jt: (0,) * a.ndim)
    kern = functools.partial(_prompt_attn_kernel, tb=tb, strip=strip, lam_init=lam_init)
    grid_spec = pltpu.PrefetchScalarGridSpec(
        num_scalar_prefetch=2,
        grid=(batch, n_pairs),
        in_specs=[pl.BlockSpec((None, N_MAPS * V_DIM, tb), lambda b, p, it, jt: (b * nb + it[p], 0, 0)),
                  pl.BlockSpec((tb, N_MAPS * V_DIM), lambda b, p, it, jt: (b * nb + jt[p], 0)),
                  pl.BlockSpec((None, N_HEADS * VE_ROWS, tb), lambda b, p, it, jt: (b * nb + jt[p], 0, 0)),
                  cst(lq1), cst(lk1), cst(lq2), cst(lk2), cst(gs_col)],
        out_specs=pl.BlockSpec((tb, ATT_W), lambda b, p, it, jt: (b * nb + it[p], 0)),
        scratch_shapes=[pltpu.VMEM((N_MAPS, 1, tb), F32),
                        pltpu.VMEM((N_MAPS, VE_ROWS, tb), F32),
                        pltpu.VMEM((tb, tb), F32)],
    )
    return pl.pallas_call(
        kern,
        grid_spec=grid_spec,
        out_shape=jax.ShapeDtypeStruct((batch * seq, ATT_W), BF16),
        compiler_params=_cparams(("arbitrary", "arbitrary")),
        name="prompt_attn",
    )(i_tab, j_tab, qt, kbe, vte, lq1, lk1, lq2, lk2, gs_col)


def _page_copies(step, sl, pt_ref, ck_hbm, cv_hbm, kbuf, vbuf, sems, pps):
    cps = []
    for pi in range(pps):
        page = pt_ref[step * pps + pi]
        cps.append(pltpu.make_async_copy(ck_hbm.at[page], kbuf.at[sl, pi], sems.at[0, sl]))
        cps.append(pltpu.make_async_copy(cv_hbm.at[page], vbuf.at[sl, pi], sems.at[1, sl]))
    return cps


def _decode_issue(s, n_steps, pt_ref, ck_hbm, cv_hbm, kbuf, vbuf, sems, pps):
    dma = (pt_ref, ck_hbm, cv_hbm, kbuf, vbuf, sems, pps)

    @pl.when(s == 0)
    def _():
        for cp in _page_copies(0, 0, *dma):
            cp.start()

    @pl.when(s + 1 < n_steps)
    def _():
        for cp in _page_copies(s + 1, 1 - s % 2, *dma):
            cp.start()


def _decode_compute(s, pt_ref, qb_ref, kn_ref, vn_ref, sl_ref, lq1_ref, lk1_ref, lq2_ref, lk2_ref,
                    gs_ref, ck_hbm, cv_hbm, o_ref, kbuf, vbuf, sems, m_scr, l_scr, acc_scr,
                    *, pps, ng, past, lam_init):
    g = s % ng
    slot = s % 2
    for cp in _page_copies(s, slot, pt_ref, ck_hbm, cv_hbm, kbuf, vbuf, sems, pps):
        cp.wait()
    kp_refs = [kbuf.at[slot, pi] for pi in range(pps)]
    vp_refs = [vbuf.at[slot, pi] for pi in range(pps)]

    @pl.when(g == 0)
    def _():
        m_scr[...] = jnp.full(m_scr.shape, NEG_INF, F32)
        l_scr[...] = jnp.zeros(l_scr.shape, F32)
        acc_scr[...] = jnp.zeros(acc_scr.shape, F32)

    q = qb_ref[...]
    slopes = sl_ref[...]
    head_of_row = lax.broadcasted_iota(jnp.int32, (N_MAPS, 1), 0) // 2
    m_run = m_scr[...]
    l_run = l_scr[...]
    acc = acc_scr[...]
    kcat = jnp.concatenate([kp_refs[pi][...] for pi in range(pps)], axis=1).astype(BF16)
    kpos = g * (pps * PAGE) + lax.broadcasted_iota(jnp.int32, (1, pps * PAGE), 1)
    dist = (past - kpos).astype(F32)
    sc = jnp.dot(q, kcat, preferred_element_type=F32) - slopes * dist
    m_new = jnp.maximum(m_run, jnp.max(sc, axis=-1, keepdims=True))
    alpha = jnp.exp2(m_run - m_new)
    p = jnp.exp2(sc - m_new)
    vcat = jnp.concatenate(
        [jnp.concatenate([vp_refs[pi][pl.ds(hh, PAGE, stride=N_HEADS), :] for pi in range(pps)], axis=0)
         for hh in range(N_HEADS)], axis=1).astype(BF16)
    r = jnp.dot(p.astype(BF16), vcat, preferred_element_type=F32)
    pv = r[:, :V_DIM]
    for hh in range(1, N_HEADS):
        pv = jnp.where(head_of_row == hh, r[:, hh * V_DIM:(hh + 1) * V_DIM], pv)
    m_scr[...] = m_new
    l_run = alpha * l_run + jnp.sum(p, axis=-1, keepdims=True)
    acc = alpha * acc + pv
    l_scr[...] = l_run
    acc_scr[...] = acc
    m_run = m_new

    @pl.when(g == ng - 1)
    def _():
        kn = kn_ref[...].astype(BF16).astype(F32)
        vn = vn_ref[...].astype(BF16).astype(F32)
        s = jnp.sum(q.astype(F32) * kn, axis=-1, keepdims=True)
        m_new = jnp.maximum(m_run, s)
        alpha = jnp.exp2(m_run - m_new)
        p = jnp.exp2(s - m_new)
        l_fin = alpha * l_run + p
        acc_fin = alpha * acc + p.astype(BF16).astype(F32) * vn
        lam = _lam_value(lq1_ref[...], lk1_ref[...], lq2_ref[...], lk2_ref[...], lam_init)
        outn = acc_fin / l_fin
        for hh in range(N_HEADS):
            o = outn[2 * hh:2 * hh + 1] - lam * outn[2 * hh + 1:2 * hh + 2]
            o_ref[:, hh * V_DIM:(hh + 1) * V_DIM] = _subln(o, gs_ref[...], lam_init)


def _decode_attn_kernel(pt_ref, *refs, pps, **static):
    s = pl.program_id(0)
    ck_hbm, cv_hbm, _, kbuf, vbuf, sems = refs[9:15]
    _decode_issue(s, pl.num_programs(0), pt_ref, ck_hbm, cv_hbm, kbuf, vbuf, sems, pps)
    _decode_compute(s, pt_ref, *refs, pps=pps, **static)


def _decode_attention(qblk, k_new, v_rows, slopes, lq1, lk1, lq2, lk2, gs, cache_kt, cache_v2,
                      page_table, lam_init):
    nb, n_pages = page_table.shape
    pps = 32
    while n_pages % pps:
        pps //= 2
    ng = n_pages // pps
    past = n_pages * PAGE
    pt_flat = page_table.reshape(-1)
    cst = lambda shape: pl.BlockSpec(shape, lambda s, pt: (0,) * len(shape))
    per_seq = lambda r, w: pl.BlockSpec((None, r, w), lambda s, pt: (s // ng, 0, 0))
    kern = functools.partial(_decode_attn_kernel, pps=pps, ng=ng, past=past, lam_init=lam_init)
    grid_spec = pltpu.PrefetchScalarGridSpec(
        num_scalar_prefetch=1,
        grid=(nb * ng,),
        in_specs=[per_seq(N_MAPS, ATT_W), per_seq(1, ATT_W), per_seq(N_MAPS, V_DIM),
                  cst(slopes.shape), cst(lq1.shape), cst(lk1.shape), cst(lq2.shape), cst(lk2.shape),
                  cst(gs.shape),
                  pl.BlockSpec(memory_space=pl.ANY), pl.BlockSpec(memory_space=pl.ANY)],
        out_specs=per_seq(1, ATT_W),
        scratch_shapes=[pltpu.VMEM((2, pps, ATT_W, PAGE), F32),
                        pltpu.VMEM((2, pps, PAGE * N_HEADS, V_DIM), F32),
                        pltpu.SemaphoreType.DMA((2, 2)),
                        pltpu.VMEM((N_MAPS, 1), F32), pltpu.VMEM((N_MAPS, 1), F32),
                        pltpu.VMEM((N_MAPS, V_DIM), F32)],
    )
    return pl.pallas_call(
        kern,
        grid_spec=grid_spec,
        out_shape=jax.ShapeDtypeStruct((nb, 1, ATT_W), F32),
        compiler_params=_cparams(("arbitrary",)),
        name="decode_attn",
    )(pt_flat, qblk, k_new, v_rows, slopes, lq1, lk1, lq2, lk2, gs, cache_kt, cache_v2)


def _fused_attn_kernel(it_ref, jt_ref, pt_ref, qt_ref, kbe_ref, vte_ref, lq1_ref, lk1_ref, lq2_ref,
                       lk2_ref, gsc_ref, qb_ref, kn_ref, vn_ref, sl_ref, gs_ref, ck_hbm, cv_hbm,
                       o_ref, od_ref, m_scr, acc_scr, mask_scr, kbuf, vbuf, sems, dm_scr, dl_scr,
                       dacc_scr, *, tb, strip, pps, ng, n_units, past, lam_init):
    pidx = pl.program_id(1)
    s = pl.program_id(0) * pl.num_programs(1) + pidx

    @pl.when(s < n_units)
    def _():
        _decode_issue(s, n_units, pt_ref, ck_hbm, cv_hbm, kbuf, vbuf, sems, pps)

    _prompt_attn_step(pidx, it_ref, jt_ref, qt_ref, kbe_ref, vte_ref, lq1_ref, lk1_ref, lq2_ref,
                      lk2_ref, gsc_ref, o_ref, m_scr, acc_scr, mask_scr, tb=tb, strip=strip,
                      lam_init=lam_init)

    @pl.when(s < n_units)
    def _():
        _decode_compute(s, pt_ref, qb_ref, kn_ref, vn_ref, sl_ref, lq1_ref, lk1_ref, lq2_ref,
                        lk2_ref, gs_ref, ck_hbm, cv_hbm, od_ref, kbuf, vbuf, sems, dm_scr, dl_scr,
                        dacc_scr, pps=pps, ng=ng, past=past, lam_init=lam_init)


def _fused_decode_pages_per_step(n_steps, nb, n_pages):
    pps = 1
    while pps <= n_pages:
        if n_pages % pps == 0 and nb * (n_pages // pps) <= n_steps:
            return pps
        pps *= 2
    return None


def _fused_attention(qt, kbe, vte, lq1, lk1, lq2, lk2, gs_col, batch, seq, tb,
                     qblk, k_new, v_rows, slopes, gs, cache_kt, cache_v2, page_table, pps, lam_init):
    nblk = seq // tb
    i_tab, j_tab, n_pairs = _attn_pairs(nblk)
    nb, n_pages = page_table.shape
    ng = n_pages // pps
    n_units = nb * ng
    assert n_units <= batch * n_pairs
    past = n_pages * PAGE
    pt_flat = page_table.reshape(-1)
    cst = lambda a: pl.BlockSpec(a.shape, lambda b, p, it, jt, pt: (0,) * a.ndim)

    def per_seq(r, w):
        def index(b, p, it, jt, pt):
            return (jnp.minimum((b * n_pairs + p) // ng, nb - 1), 0, 0)
        return pl.BlockSpec((None, r, w), index)

    kern = functools.partial(_fused_attn_kernel, tb=tb, strip=tb, pps=pps, ng=ng, n_units=n_units,
                             past=past, lam_init=lam_init)
    grid_spec = pltpu.PrefetchScalarGridSpec(
        num_scalar_prefetch=3,
        grid=(batch, n_pairs),
        in_specs=[pl.BlockSpec((None, N_MAPS * V_DIM, tb),
                               lambda b, p, it, jt, pt: (b * nblk + it[p], 0, 0)),
                  pl.BlockSpec((tb, N_MAPS * V_DIM), lambda b, p, it, jt, pt: (b * nblk + jt[p], 0)),
                  pl.BlockSpec((None, N_HEADS * VE_ROWS, tb),
                               lambda b, p, it, jt, pt: (b * nblk + jt[p], 0, 0)),
                  cst(lq1), cst(lk1), cst(lq2), cst(lk2), cst(gs_col),
                  per_seq(N_MAPS, ATT_W), per_seq(1, ATT_W), per_seq(N_MAPS, V_DIM),
                  cst(slopes), cst(gs),
                  pl.BlockSpec(memory_space=pl.ANY), pl.BlockSpec(memory_space=pl.ANY)],
        out_specs=(pl.BlockSpec((tb, ATT_W), lambda b, p, it, jt, pt: (b * nblk + it[p], 0)),
                   per_seq(1, ATT_W)),
        scratch_shapes=[pltpu.VMEM((N_MAPS, 1, tb), F32),
                        pltpu.VMEM((N_MAPS, VE_ROWS, tb), F32),
                        pltpu.VMEM((tb, tb), F32),
                        pltpu.VMEM((2, pps, ATT_W, PAGE), F32),
                        pltpu.VMEM((2, pps, PAGE * N_HEADS, V_DIM), F32),
                        pltpu.SemaphoreType.DMA((2, 2)),
                        pltpu.VMEM((N_MAPS, 1), F32), pltpu.VMEM((N_MAPS, 1), F32),
                        pltpu.VMEM((N_MAPS, V_DIM), F32)],
    )
    return pl.pallas_call(
        kern,
        grid_spec=grid_spec,
        out_shape=(jax.ShapeDtypeStruct((batch * seq, ATT_W), BF16),
                   jax.ShapeDtypeStruct((nb, 1, ATT_W), F32)),
        compiler_params=_cparams(("arbitrary", "arbitrary")),
        name="fused_attn",
    )(i_tab, j_tab, pt_flat, qt, kbe, vte, lq1, lk1, lq2, lk2, gs_col,
      qblk, k_new, v_rows, slopes, gs, cache_kt, cache_v2)


def _expm1(y):
    acc = 1.0 + y * (1.0 / 10.0)
    for n in range(9, 1, -1):
        acc = 1.0 + (y * (1.0 / n)) * acc
    return jnp.where(jnp.abs(y) < 0.125, y * acc, jnp.exp(y) - 1.0)


def _log1p(z):
    return jnp.where(z < 1e-4, z * (1.0 - z * (0.5 - z * (1.0 / 3.0))), jnp.log(1.0 + z))


def _rg_gates(xc, wa_ref, ba_ref, wx_ref, bx_ref, lam_ref):
    xcb = xc.astype(BF16)
    r = _sigmoid(jnp.dot(xcb, wa_ref[...], preferred_element_type=F32) + ba_ref[...])
    ig = _sigmoid(jnp.dot(xcb, wx_ref[...], preferred_element_type=F32) + bx_ref[...])
    nl = -lam_ref[...]
    sp = jnp.maximum(nl, 0.0) + _log1p(jnp.exp(-jnp.abs(nl)))
    log_a = -RG_C * r * sp
    a = jnp.exp(log_a)
    bt = jnp.sqrt(-_expm1(2.0 * log_a)) * (ig * xc)
    return a, bt


def _rglru_seq_kernel(xr_ref, gr_ref, cw_ref, cb_ref, wa_ref, ba_ref, wx_ref, bx_ref, lam_ref,
                      rg_ref, ctail_ref, htail_ref, prev_scr, h_scr, *, ts):
    s = pl.program_id(1)

    @pl.when(s == 0)
    def _():
        prev_scr[...] = jnp.zeros(prev_scr.shape, F32)
        h_scr[...] = jnp.zeros(h_scr.shape, F32)

    x = xr_ref[...]
    prev = prev_scr[...]
    row8 = lax.broadcasted_iota(jnp.int32, (SUBLANES, 1), 0)
    cw = cw_ref[...]
    xc = cb_ref[...] + x * cw[CONV_W - 1:CONV_W]
    for jj in range(1, CONV_W):
        xs = pltpu.roll(x, jj, axis=0)
        ps = pltpu.roll(prev, jj, axis=0)
        top = jnp.where(row8 < jj, ps, xs[:SUBLANES])
        xs = jnp.concatenate([top, xs[SUBLANES:]], axis=0)
        xc = xc + xs * cw[CONV_W - 1 - jj:CONV_W - jj]
    a, bt = _rg_gates(xc, wa_ref, ba_ref, wx_ref, bx_ref, lam_ref)
    grp = min(32, ts)
    row_in_grp = lax.broadcasted_iota(jnp.int32, (ts, 1), 0) % grp
    d = 1
    while d < grp:
        a_s = pltpu.roll(a, d, axis=0)
        b_s = pltpu.roll(bt, d, axis=0)
        keep = row_in_grp >= d
        bt = jnp.where(keep, a * b_s + bt, bt)
        a = jnp.where(keep, a * a_s, a)
        d *= 2
    carry = h_scr[...]
    parts = []
    for g0 in range(0, ts, grp):
        hg = a[g0:g0 + grp] * carry + bt[g0:g0 + grp]
        parts.append(hg)
        carry = hg[grp - 1:grp]
    h = jnp.concatenate(parts, axis=0)
    rg_ref[...] = (h * jax.nn.gelu(gr_ref[...])).astype(rg_ref.dtype)
    h_scr[...] = h[ts - 1:ts]
    prev_scr[...] = x[ts - SUBLANES:ts]
    ctail_ref[...] = x[ts - SUBLANES:ts]
    htail_ref[...] = h[ts - SUBLANES:ts]


def _rglru_seq(xr, gr, cw, cb, wa_bd, ba, wx_bd, bx, lam, batch, seq):
    ts = min(256, seq)
    ns = seq // ts
    kern = functools.partial(_rglru_seq_kernel, ts=ts)
    cst = lambda a: pl.BlockSpec(a.shape, lambda b, s: (0,) * a.ndim)
    tile = pl.BlockSpec((ts, RG_W), lambda b, s: (b * ns + s, 0))
    tail = pl.BlockSpec((None, SUBLANES, RG_W), lambda b, s: (b, 0, 0))
    return pl.pallas_call(
        kern,
        grid=(batch, ns),
        in_specs=[tile, tile, cst(cw), cst(cb), cst(wa_bd), cst(ba), cst(wx_bd), cst(bx), cst(lam)],
        out_specs=(tile, tail, tail),
        out_shape=(jax.ShapeDtypeStruct((batch * seq, RG_W), BF16),
                   jax.ShapeDtypeStruct((batch, SUBLANES, RG_W), F32),
                   jax.ShapeDtypeStruct((batch, SUBLANES, RG_W), F32)),
        scratch_shapes=[pltpu.VMEM((SUBLANES, RG_W), F32), pltpu.VMEM((1, RG_W), F32)],
        compiler_params=_cparams(("parallel", "arbitrary")),
        name="rglru_seq",
    )(xr, gr, cw, cb, wa_bd, ba, wx_bd, bx, lam)


def _rglru_step_kernel(xr_ref, gr_ref, c0_ref, c1_ref, c2_ref, h0_ref, cw_ref, cb_ref, wa_ref, ba_ref,
                       wx_ref, bx_ref, lam_ref, rg_ref, h_ref):
    x = xr_ref[...]
    cw = cw_ref[...]
    xc = cb_ref[...] + c0_ref[...] * cw[0:1]
    xc = xc + c1_ref[...] * cw[1:2]
    xc = xc + c2_ref[...] * cw[2:3]
    xc = xc + x * cw[3:4]
    a, bt = _rg_gates(xc, wa_ref, ba_ref, wx_ref, bx_ref, lam_ref)
    h = a * h0_ref[...] + bt
    h_ref[...] = h
    rg_ref[...] = (h * jax.nn.gelu(gr_ref[...])).astype(rg_ref.dtype)


def _rglru_step(xr, gr, conv_state, h0, cw, cb, wa_bd, ba, wx_bd, bx, lam):
    nb = xr.shape[0]
    args = (xr, gr, conv_state[:, 0], conv_state[:, 1], conv_state[:, 2], h0,
            cw, cb, wa_bd, ba, wx_bd, bx, lam)
    return pl.pallas_call(
        _rglru_step_kernel,
        grid=(1,),
        in_specs=[_full(a.shape) for a in args],
        out_specs=(_full((nb, RG_W)), _full((nb, RG_W))),
        out_shape=(jax.ShapeDtypeStruct((nb, RG_W), BF16), jax.ShapeDtypeStruct((nb, RG_W), F32)),
        compiler_params=_cparams(("arbitrary",)),
        name="rglru_step",
    )(*args)


def _outproj_router_kernel(att_ref, rg_ref, x_ref, wo_ref, g2_ref, wr_ref, rb_ref, tri_ref,
                           x1_ref, hm8_ref, hmb_ref, idx_ref, gate_ref, rank_ref, cnt_ref,
                           carry_scr, *, tm, tiles_per_super):
    i = pl.program_id(0)

    @pl.when(i % tiles_per_super == 0)
    def _():
        carry_scr[...] = jnp.zeros(carry_scr.shape, F32)

    mixed = (jnp.dot(att_ref[...], wo_ref[:ATT_W, :], preferred_element_type=F32)
             + jnp.dot(rg_ref[...], wo_ref[ATT_W:, :], preferred_element_type=F32))
    x1 = x_ref[...] + mixed
    x1_ref[...] = x1
    ms = jnp.mean(x1 * x1, axis=-1, keepdims=True)
    hm = x1 * lax.rsqrt(ms + EPS) * g2_ref[...]
    hmb = hm.astype(BF16)
    hmb_ref[...] = hmb
    for s in range(SUBLANES):
        hm8_ref[pl.ds(s, tm, stride=SUBLANES), :] = hm[:, s * LANES:(s + 1) * LANES]

    scores = _sigmoid(_nt_dot(wr_ref[...], hmb))
    choice = scores + rb_ref[...]
    ch3 = choice.reshape(N_GROUPS, GROUP_SIZE, tm)
    i8 = lax.broadcasted_iota(jnp.int32, (N_GROUPS, GROUP_SIZE, tm), 1)
    m1 = jnp.max(ch3, axis=1, keepdims=True)
    f1 = jnp.min(jnp.where(ch3 == m1, i8, GROUP_SIZE), axis=1, keepdims=True)
    m2 = jnp.max(jnp.where(i8 == f1, -jnp.inf, ch3), axis=1, keepdims=True)
    gs3 = m1 + m2
    gs = gs3.reshape(N_GROUPS, tm)
    gi = lax.broadcasted_iota(jnp.int32, (N_GROUPS, tm), 0)
    grank = jnp.zeros((N_GROUPS, tm), jnp.int32)
    for g2 in range(N_GROUPS):
        rowv = gs[g2:g2 + 1, :]
        beats = (rowv > gs) | ((rowv == gs) & (g2 < gi))
        grank = grank + beats.astype(jnp.int32)
    gsel = (grank < TOPK_GROUPS).reshape(N_GROUPS, 1, tm)
    masked = jnp.where(gsel, ch3, NEG_INF).reshape(N_EXPERTS, tm)

    ei = lax.broadcasted_iota(jnp.int32, (N_EXPERTS, tm), 0)
    picks, gvals = [], []
    sel = jnp.zeros((N_EXPERTS, tm), F32)
    for _ in range(TOP_K):
        mx = jnp.max(masked, axis=0, keepdims=True)
        first = jnp.min(jnp.where(masked == mx, ei, N_EXPERTS), axis=0, keepdims=True)
        hit = ei == first
        picks.append(first)
        gvals.append(jnp.sum(jnp.where(hit, scores, 0.0), axis=0, keepdims=True))
        sel = jnp.where(hit, 1.0, sel)
        masked = jnp.where(hit, -jnp.inf, masked)
    gsum = gvals[0]
    for kk in range(1, TOP_K):
        gsum = gsum + gvals[kk]

    pos = jnp.dot(sel.astype(BF16), tri_ref[...], preferred_element_type=F32) + carry_scr[...]
    for kk in range(TOP_K):
        idx_ref[kk:kk + 1, :] = picks[kk]
        gate_ref[kk:kk + 1, :] = gvals[kk] / gsum * ROUTE_SCALE
        rk = jnp.sum(jnp.where(ei == picks[kk], pos, 0.0), axis=0, keepdims=True)
        rank_ref[kk:kk + 1, :] = rk.astype(jnp.int32)
    carry = carry_scr[...] + jnp.sum(sel, axis=1, keepdims=True)
    carry_scr[...] = carry
    cnt_ref[...] = jnp.broadcast_to(carry, cnt_ref.shape)


def _outproj_router(att, rg, x2, wo_bf, g2, wr_t, rb_col, tt):
    T, D = x2.shape
    tm = min(512, T)
    tiles_per_super = tt // tm
    n_super = T // tt
    tri = jnp.triu(jnp.ones((tm, tm), BF16), k=1)
    kern = functools.partial(_outproj_router_kernel, tm=tm, tiles_per_super=tiles_per_super)
    row = lambda w: pl.BlockSpec((tm, w), lambda i: (i, 0))
    col = pl.BlockSpec((TOP_K, tm), lambda i: (0, i))
    return pl.pallas_call(
        kern,
        grid=(T // tm,),
        in_specs=[row(ATT_W), row(RG_W), row(D), _full(wo_bf.shape), _full(g2.shape),
                  _full(wr_t.shape), _full(rb_col.shape), _full(tri.shape)],
        out_specs=(row(D), pl.BlockSpec((tm * SUBLANES, LANES), lambda i: (i, 0)), row(D),
                   col, col, col,
                   pl.BlockSpec((None, N_EXPERTS, LANES), lambda i: (i // tiles_per_super, 0, 0))),
        out_shape=(jax.ShapeDtypeStruct((T, D), F32),
                   jax.ShapeDtypeStruct((T * SUBLANES, LANES), F32),
                   jax.ShapeDtypeStruct((T, D), BF16),
                   jax.ShapeDtypeStruct((TOP_K, T), jnp.int32),
                   jax.ShapeDtypeStruct((TOP_K, T), F32),
                   jax.ShapeDtypeStruct((TOP_K, T), jnp.int32),
                   jax.ShapeDtypeStruct((n_super, N_EXPERTS, LANES), F32)),
        scratch_shapes=[pltpu.VMEM((N_EXPERTS, 1), F32)],
        compiler_params=_cparams(("arbitrary",)),
        name="outproj_router",
    )(att, rg, x2, wo_bf, g2, wr_t, rb_col, tri)


def _moe_kernel(cnt_ref, off_ref, hm8_ref, w13_ref, w2_ref, idx_ref, rank_ref, gate_ref, y8_ref,
                dest_v, dest_s, gate_s, row_s, gsort_s, xg, og, sem, *, tt, mch, d_model, d_exp):
    ti = pl.program_id(0)
    e = pl.program_id(1)
    chunks = d_model // LANES

    def rows_at(off):
        return pl.ds(pl.multiple_of(off, chunks), chunks)

    @pl.when(e == 0)
    def _():
        idx = idx_ref[...]
        dest = rank_ref[...]
        for ee in range(N_EXPERTS):
            dest = dest + jnp.where(idx == ee, off_ref[ti * N_EXPERTS + ee], 0)
        dest_v[...] = dest
        copies = []
        for kk in range(TOP_K):
            span = pl.ds(kk * tt, tt)
            copies.append(pltpu.make_async_copy(dest_v.at[kk], dest_s.at[span], sem.at[0]))
            copies.append(pltpu.make_async_copy(gate_ref.at[kk], gate_s.at[span], sem.at[1]))
        for cp in copies:
            cp.start()
        y8_ref[...] = jnp.zeros(y8_ref.shape, F32)
        xg[...] = jnp.zeros(xg.shape, F32)
        og[...] = jnp.zeros(og.shape, F32)
        for cp in copies:
            cp.wait()

        def invert(io, _):
            for u in range(SUBLANES):
                t = io * SUBLANES + u
                picks = [(dest_s[kk * tt + t], gate_s[kk * tt + t]) for kk in range(TOP_K)]
                for slot, gt in picks:
                    row_s[slot] = t * chunks
                    gsort_s[slot] = gt
            return 0

        lax.fori_loop(0, tt // SUBLANES, invert, 0)

    c = cnt_ref[ti * N_EXPERTS + e]
    off = off_ref[ti * N_EXPERTS + e]
    nch = (c + mch - 1) // mch
    g_unroll = 32
    s_unroll = 8

    def chunk(ch, carry):
        base = off + ch * mch
        nvalid = jnp.minimum(c - ch * mch, mch)

        def gather_row(ii):
            xg[rows_at(ii * chunks), :] = hm8_ref[rows_at(row_s[base + ii]), :]

        def gather_group(io, _):
            for u in range(g_unroll):
                gather_row(io * g_unroll + u)
            return 0

        def gather_one(ii, _):
            gather_row(ii)
            return 0

        n_full = nvalid // g_unroll
        lax.fori_loop(0, n_full, gather_group, 0)
        lax.fori_loop(n_full * g_unroll, nvalid, gather_one, 0)
        xb = jnp.concatenate([xg[pl.ds(s, mch, stride=chunks), :] for s in range(chunks)],
                             axis=-1).astype(BF16)
        h13 = jnp.dot(xb, w13_ref[...], preferred_element_type=F32)
        h1 = h13[:, :d_exp]
        hdn = (h1 * _sigmoid(h1)) * h13[:, d_exp:]
        o = jnp.dot(hdn.astype(BF16), w2_ref[...], preferred_element_type=F32)
        for s in range(chunks):
            og[pl.ds(s, mch, stride=chunks), :] = o[:, s * LANES:(s + 1) * LANES]

        def scatter_rows(first, n):
            vals = []
            for u in range(n):
                ii = first + u
                r = row_s[base + ii]
                gt = gsort_s[base + ii]
                vals.append((r, y8_ref[rows_at(r), :] + gt * og[rows_at(ii * chunks), :]))
            for r, v in vals:
                y8_ref[rows_at(r), :] = v

        def scatter_group(io, _):
            scatter_rows(io * s_unroll, s_unroll)
            return 0

        def scatter_one(ii, _):
            scatter_rows(ii, 1)
            return 0

        n_full_s = nvalid // s_unroll
        lax.fori_loop(0, n_full_s, scatter_group, 0)
        lax.fori_loop(n_full_s * s_unroll, nvalid, scatter_one, 0)
        return carry

    lax.fori_loop(0, nch, chunk, 0)


def _moe(hm8, w13, w2, idx, rank, gate, cnt_flat, off_flat, tt, mch):
    T8, _ = hm8.shape
    T = T8 // SUBLANES
    n_tiles = T // tt
    d_model = w13.shape[1]
    d_exp = w2.shape[1]
    assert d_model == SUBLANES * LANES and tt & (tt - 1) == 0 and mch % SUBLANES == 0
    kern = functools.partial(_moe_kernel, tt=tt, mch=mch, d_model=d_model, d_exp=d_exp)
    single = dict(pipeline_mode=pl.Buffered(1))
    picks = pl.BlockSpec((TOP_K, tt), lambda t, e, c, o: (0, t))
    grid_spec = pltpu.PrefetchScalarGridSpec(
        num_scalar_prefetch=2,
        grid=(n_tiles, N_EXPERTS),
        in_specs=[pl.BlockSpec((tt * SUBLANES, LANES), lambda t, e, c, o: (t, 0), **single),
                  pl.BlockSpec((None, d_model, 2 * d_exp), lambda t, e, c, o: (e, 0, 0)),
                  pl.BlockSpec((None, d_exp, d_model), lambda t, e, c, o: (e, 0, 0)),
                  picks, picks, picks],
        out_specs=pl.BlockSpec((tt * SUBLANES, LANES), lambda t, e, c, o: (t, 0), **single),
        scratch_shapes=[pltpu.VMEM((TOP_K, tt), jnp.int32),
                        pltpu.SMEM((tt * TOP_K,), jnp.int32), pltpu.SMEM((tt * TOP_K,), F32),
                        pltpu.SMEM((tt * TOP_K,), jnp.int32), pltpu.SMEM((tt * TOP_K,), F32),
                        pltpu.VMEM((mch * SUBLANES, LANES), F32),
                        pltpu.VMEM((mch * SUBLANES, LANES), F32),
                        pltpu.SemaphoreType.DMA((2,))],
    )
    return pl.pallas_call(
        kern,
        grid_spec=grid_spec,
        out_shape=jax.ShapeDtypeStruct((T8, LANES), F32),
        compiler_params=_cparams(("arbitrary", "arbitrary")),
        name="moe_experts",
    )(cnt_flat, off_flat, hm8, w13, w2, idx, rank, gate)


def _final_kernel(y8_ref, x1_ref, hmb_ref, ws13_ref, ws2_ref, gp_ref, wg_ref, p_ref, wp_ref, o_ref,
                  *, tm, d_sh):
    chunks = x1_ref.shape[1] // LANES
    routed = jnp.concatenate([y8_ref[pl.ds(s, tm, stride=chunks), :] for s in range(chunks)], axis=-1)
    hmb = hmb_ref[...]
    h13 = jnp.dot(hmb, ws13_ref[...], preferred_element_type=F32)
    h1 = h13[:, :d_sh]
    hdn = (h1 * _sigmoid(h1)) * h13[:, d_sh:]
    shared = jnp.dot(hdn.astype(BF16), ws2_ref[...], preferred_element_type=F32)
    x2 = x1_ref[...] + (routed + shared)
    ms = jnp.mean(x2 * x2, axis=-1, keepdims=True)
    hp = (x2 * lax.rsqrt(ms + EPS) * gp_ref[...]).astype(BF16)
    gate = _sigmoid(jnp.dot(hp, wg_ref[...], preferred_element_type=F32))
    proj = jnp.dot(p_ref[...].astype(BF16), wp_ref[...], preferred_element_type=F32)
    o_ref[...] = x2 + gate * proj


def _final(y8, x1, hmb, ws13, ws2, gp, wg, p2, wp):
    T, D = x1.shape
    tm = min(512, T)
    d_sh = ws2.shape[0]
    kern = functools.partial(_final_kernel, tm=tm, d_sh=d_sh)
    row = lambda w: pl.BlockSpec((tm, w), lambda i: (i, 0))
    return pl.pallas_call(
        kern,
        grid=(T // tm,),
        in_specs=[pl.BlockSpec((tm * SUBLANES, LANES), lambda i: (i, 0)), row(D), row(D),
                  _full(ws13.shape), _full(ws2.shape), _full(gp.shape), _full(wg.shape),
                  row(p2.shape[1]), _full(wp.shape)],
        out_specs=row(D),
        out_shape=jax.ShapeDtypeStruct((T, D), F32),
        compiler_params=_cparams(("parallel",)),
        name="final",
    )(y8, x1, hmb, ws13, ws2, gp, wg, p2, wp)


def _block_diag(w):
    nh, d, _ = w.shape
    eye = jnp.eye(nh, dtype=w.dtype)
    return (eye[:, None, :, None] * w[:, :, None, :]).reshape(nh * d, nh * d)


def _prep_weights(i, g_norm1, w_in, g_q, g_k, lam_q1, lam_k1, lam_q2, lam_k2, g_subln, conv_w, conv_b,
                  w_a, b_a, w_x, b_x, lam_rg, w_out, g_norm2, w_router, router_bias, w_e1, w_e3, w_e2,
                  w_s1, w_s3, w_s2, g_ple, w_ple_gate, w_ple_proj):
    return dict(
        g1=g_norm1[i][None], w_in=w_in[i].astype(BF16),
        wqt=w_in[i][:, :ATT_W].T.astype(BF16), wkt=w_in[i][:, ATT_W:2 * ATT_W].T.astype(BF16),
        wvt=w_in[i][:, 2 * ATT_W:3 * ATT_W].T.astype(BF16),
        gq_col=g_q[i][:, None], gk_col=g_k[i][:, None],
        lq1=lam_q1[i][None], lk1=lam_k1[i][None], lq2=lam_q2[i][None], lk2=lam_k2[i][None],
        gs=g_subln[i][None], gs_col=g_subln[i][:, None],
        cw=conv_w[i], cb=conv_b[i][None],
        wa=_block_diag(w_a[i]).astype(BF16), ba=b_a[i][None],
        wx=_block_diag(w_x[i]).astype(BF16), bx=b_x[i][None], lam_rg=lam_rg[i][None],
        wo=w_out[i].astype(BF16), g2=g_norm2[i][None],
        wr_t=w_router[i].T.astype(BF16), rb=router_bias[i][:, None],
        w13=jnp.concatenate([w_e1[i], w_e3[i]], axis=-1).astype(BF16), w2=w_e2[i].astype(BF16),
        ws13=jnp.concatenate([w_s1[i], w_s3[i]], axis=-1).astype(BF16), ws2=w_s2[i].astype(BF16),
        gp=g_ple[i][None], wg=w_ple_gate[i].astype(BF16), wp=w_ple_proj[i].astype(BF16),
    )


def _moe_stage(att, rg, x2, p2, W, tt, mch):
    T = x2.shape[0]
    x1, hm8, hmb, idx, gate, rank, cnt = _outproj_router(att, rg, x2, W['wo'], W['g2'], W['wr_t'],
                                                         W['rb'], tt)
    cnt_i = cnt[:, :, 0].astype(jnp.int32)
    off_i = jnp.cumsum(cnt_i, axis=1) - cnt_i
    y8 = _moe(hm8, W['w13'], W['w2'], idx, rank, gate, cnt_i.reshape(-1), off_i.reshape(-1), tt, mch)
    return _final(y8, x1, hmb, W['ws13'], W['ws2'], W['gp'], W['wg'], p2, W['wp'])


def _slopes_col():
    h = jnp.arange(N_MAPS) // 2 + 1
    return (LOG2E * jnp.exp2(-8.0 * h.astype(F32) / N_HEADS))[:, None]


def kernel(x_prompt, x_sample, cache_k, cache_v, state_conv, state_h, page_table, p_prompt, p_sample, g_norm1, w_in, g_q, g_k, lam_q1, lam_k1, lam_q2, lam_k2, g_subln, conv_w, conv_b, w_a, b_a, w_x, b_x, lam_rg, w_out, g_norm2, w_router, router_bias, w_e1, w_e3, w_e2, w_s1, w_s3, w_s2, g_ple, w_ple_gate, w_ple_proj):
    depth = w_in.shape[0]
    B, S, D = x_prompt.shape
    NB, NS, _ = x_sample.shape
    assert NS == 1
    n_pool = cache_k.shape[1]
    yp = x_prompt.reshape(B * S, D)
    ys = x_sample.reshape(NB * NS, D)
    kp_l, vp_l, cp_l, hp_l, ks_l, vs_l, cs_l, hs_l = [], [], [], [], [], [], [], []
    for i in range(depth):
        W = _prep_weights(i, g_norm1, w_in, g_q, g_k, lam_q1, lam_k1, lam_q2, lam_k2, g_subln, conv_w,
                          conv_b, w_a, b_a, w_x, b_x, lam_rg, w_out, g_norm2, w_router, router_bias,
                          w_e1, w_e3, w_e2, w_s1, w_s3, w_s2, g_ple, w_ple_gate, w_ple_proj)
        lam_init = 0.8 - 0.6 * math.exp(-0.3 * i)
        lam_vecs = (W['lq1'], W['lk1'], W['lq2'], W['lk2'])
        inproj_w = (W['g1'], W['w_in'], W['wqt'], W['wkt'], W['wvt'], W['gq_col'], W['gk_col'])

        tb = min(512, S)
        qt, kt, kbe, v4, vte, xr, gr = _inproj(yp, B, tb, *inproj_w)
        qt_s, kt_s, _, v4_s, _, xr_s, gr_s = _inproj(ys, 1, PAGE, *inproj_w)
        k_s = kt_s[0].T
        v_s = v4_s.reshape(NB, N_HEADS, V_DIM)
        qm = qt_s[0].T.reshape(NB, N_MAPS, V_DIM)[:, :, :HEAD_DIM]
        msel = jnp.eye(N_MAPS, dtype=BF16)
        qblk = (qm[:, :, None, :] * msel[None, :, :, None]).reshape(NB, N_MAPS, ATT_W)
        v_rows = jnp.repeat(v_s, 2, axis=1)
        cache_kt = jnp.transpose(cache_k[i], (0, 2, 3, 4, 1)).reshape(n_pool, ATT_W, PAGE)
        cache_v2 = cache_v[i].reshape(n_pool, PAGE * N_HEADS, V_DIM)
        decode_args = (qblk, k_s.reshape(NB, 1, ATT_W), v_rows, _slopes_col())

        nblk = S // tb
        pps = _fused_decode_pages_per_step(B * (nblk * (nblk + 1) // 2), NB, page_table.shape[1])
        if pps is not None and pps <= 32:
            att, att_s = _fused_attention(qt, kbe, vte, *lam_vecs, W['gs_col'], B, S, tb,
                                          *decode_args, W['gs'], cache_kt, cache_v2, page_table, pps,
                                          lam_init)
        else:
            att = _prompt_attention(qt, kbe, vte, *lam_vecs, W['gs_col'], B, S, tb, lam_init)
            att_s = _decode_attention(*decode_args, *lam_vecs, W['gs'], cache_kt, cache_v2,
                                      page_table, lam_init)

        rg, ctail, htail = _rglru_seq(xr, gr, W['cw'], W['cb'], W['wa'], W['ba'], W['wx'], W['bx'],
                                      W['lam_rg'], B, S)
        tt_p = min(4096, B * S)
        mch_p = tt_p // SUBLANES + tt_p // 64
        yp = _moe_stage(att, rg, yp, p_prompt[i].reshape(B * S, -1), W, tt_p, mch_p)
        kp_l.append(jnp.transpose(kt.reshape(B, N_HEADS, 2, HEAD_DIM, S), (0, 4, 1, 2, 3)))
        vp_l.append(v4.reshape(B, S, N_HEADS, V_DIM))
        cp_l.append(ctail[:, SUBLANES - (CONV_W - 1):])
        hp_l.append(htail[:, SUBLANES - 1])

        rg_s, h_s = _rglru_step(xr_s, gr_s, state_conv[i], state_h[i], W['cw'], W['cb'], W['wa'],
                                W['ba'], W['wx'], W['bx'], W['lam_rg'])
        ys = _moe_stage(att_s.reshape(NB, ATT_W).astype(BF16), rg_s, ys,
                        p_sample[i].reshape(NB * NS, -1), W, NB, 32)
        ks_l.append(k_s.reshape(NB, NS, N_HEADS, 2, HEAD_DIM))
        vs_l.append(v_s.reshape(NB, NS, N_HEADS, V_DIM))
        cs_l.append(jnp.concatenate([state_conv[i][:, 1:], xr_s[:, None, :]], axis=1))
        hs_l.append(h_s)
    return (yp.reshape(B, S, D), ys.reshape(NB, NS, D), jnp.stack(kp_l), jnp.stack(vp_l),
            jnp.stack(cp_l), jnp.stack(hp_l), jnp.stack(ks_l), jnp.stack(vs_l), jnp.stack(cs_l),
            jnp.stack(hs_l))
```
